```python
import jax, jax.numpy as jnp
from jax import lax
import numpy as np

D_MODEL = 1024
BATCH = 16
SEQ = 2048
DEPTH = 1

N_META = 16
N_HEADS = 16
HEAD_DIM = 64
D_ATTN = N_HEADS * HEAD_DIM
D_CONV = D_MODEL
CONV_WIDTH = 31
D_FF = 4 * D_MODEL
Q_BLOCK = 128
RMS_EPS = 1e-6
LN_EPS = 1e-5

PART_SIZES = [D_ATTN, D_ATTN, D_ATTN, N_HEADS, 2 * D_CONV, D_MODEL, D_MODEL]
PART_OFFSETS = [int(o) for o in np.cumsum([0] + PART_SIZES)]
N_IN = PART_OFFSETS[-1]

kernel_name = "fox_conformer_parallel_gated_hybrid"


def rms_norm(x, g):
    xf = x.astype(jnp.float32)
    y = xf * lax.rsqrt(jnp.mean(xf * xf, axis=-1, keepdims=True) + RMS_EPS)
    return (y * g.astype(jnp.float32)).astype(x.dtype)


def layer_norm(x, g, b):
    xf = x.astype(jnp.float32)
    mu = jnp.mean(xf, axis=-1, keepdims=True)
    xc = xf - mu
    var = jnp.mean(xc * xc, axis=-1, keepdims=True)
    y = xc * lax.rsqrt(var + LN_EPS) * g.astype(jnp.float32) + b.astype(jnp.float32)
    return y.astype(x.dtype)


def forgetting_attention(q, k, v, fg_logit):
    T = q.shape[1]
    scale = HEAD_DIM ** -0.5
    log_f = jax.nn.log_sigmoid(fg_logit.astype(jnp.float32))
    cum = jnp.transpose(jnp.cumsum(log_f, axis=1), (0, 2, 1))
    starts = [0] + list(range(N_META, T, Q_BLOCK))
    ends = starts[1:] + [T]
    outs = []
    for q0, q1 in zip(starts, ends):
        s = jnp.einsum('bqhd,bkhd->bhqk', q[:, q0:q1], k[:, :q1]).astype(jnp.float32) * scale
        s = s + cum[:, :, q0:q1, None] - cum[:, :, None, :q1]
        causal = jnp.arange(q0, q1)[:, None] >= jnp.arange(q1)[None, :]
        s = jnp.where(causal, s, -jnp.inf)
        p = jax.nn.softmax(s, axis=-1).astype(v.dtype)
        outs.append(jnp.einsum('bhqk,bkhd->bqhd', p, v[:, :q1]))
    return jnp.concatenate(outs, axis=1)


def causal_depthwise_conv(u, w, b):
    C = u.shape[-1]
    y = lax.conv_general_dilated(
        u, w[:, None, :].astype(u.dtype), window_strides=(1,),
        padding=((CONV_WIDTH - 1, 0),), dimension_numbers=('NWC', 'WIO', 'NWC'),
        feature_group_count=C)
    return y + b.astype(u.dtype)


def setup_inputs(seed: int = 0) -> dict:
    key = jax.random.key(seed)
    ks = jax.random.split(key, 20)
    nrm = lambda k, shape, fan_in: jax.random.normal(k, shape, jnp.float32) * (fan_in ** -0.5)
    gain = lambda k, shape: 1.0 + 0.02 * jax.random.normal(k, shape, jnp.float32)
    small = lambda k, shape: 0.02 * jax.random.normal(k, shape, jnp.float32)
    return {
        "x": jax.random.normal(ks[0], (BATCH, SEQ, D_MODEL), jnp.float32),
        "meta_tokens": jax.random.normal(ks[1], (N_META, D_MODEL), jnp.float32),
        "norm_mix_gain": gain(ks[2], (DEPTH, D_MODEL)),
        "w_in": nrm(ks[3], (DEPTH, D_MODEL, N_IN), D_MODEL),
        "b_forget": jax.random.uniform(ks[4], (DEPTH, N_HEADS), jnp.float32, 1.0, 6.0),
        "w_attn_out": nrm(ks[5], (DEPTH, D_ATTN, D_MODEL), D_ATTN),
        "b_glu": small(ks[6], (DEPTH, 2 * D_CONV)),
        "conv_dw_w": nrm(ks[7], (DEPTH, CONV_WIDTH, D_CONV), CONV_WIDTH),
        "conv_dw_b": small(ks[8], (DEPTH, D_CONV)),
        "conv_ln_gain": gain(ks[9], (DEPTH, D_CONV)),
        "conv_ln_bias": small(ks[10], (DEPTH, D_CONV)),
        "w_conv_out": nrm(ks[11], (DEPTH, D_CONV, D_MODEL), D_CONV),
        "b_conv_out": small(ks[12], (DEPTH, D_MODEL)),
        "w_out": nrm(ks[13], (DEPTH, D_MODEL, D_MODEL), D_MODEL),
        "norm_mlp_gain": gain(ks[14], (DEPTH, D_MODEL)),
        "w_mlp_up": nrm(ks[15], (DEPTH, D_MODEL, D_FF), D_MODEL),
        "w_mlp_down": nrm(ks[16], (DEPTH, D_FF, D_MODEL), D_FF),
        "final_norm_gain": gain(ks[17], (D_MODEL,)),
    }


def reference(x, meta_tokens, norm_mix_gain, w_in, b_forget, w_attn_out, b_glu,
              conv_dw_w, conv_dw_b, conv_ln_gain, conv_ln_bias, w_conv_out, b_conv_out,
              w_out, norm_mlp_gain, w_mlp_up, w_mlp_down, final_norm_gain):
    B = x.shape[0]
    meta = jnp.broadcast_to(meta_tokens[None].astype(x.dtype), (B, N_META, D_MODEL))
    h_res = jnp.concatenate([meta, x], axis=1)
    T = h_res.shape[1]
    o = PART_OFFSETS
    for l in range(DEPTH):
        hn = rms_norm(h_res, norm_mix_gain[l])
        w_l = w_in[l]
        part = lambda i: hn @ w_l[:, o[i]:o[i + 1]]
        q = part(0).reshape(B, T, N_HEADS, HEAD_DIM)
        k = part(1).reshape(B, T, N_HEADS, HEAD_DIM)
        v = part(2).reshape(B, T, N_HEADS, HEAD_DIM)
        fg = part(3) + b_forget[l]
        glu_in = part(4) + b_glu[l]
        gate_attn = jax.nn.sigmoid(part(5))
        gate_conv = jax.nn.sigmoid(part(6))

        a = forgetting_attention(q, k, v, fg).reshape(B, T, D_ATTN) @ w_attn_out[l]

        u = glu_in[..., :D_CONV] * jax.nn.sigmoid(glu_in[..., D_CONV:])
        c = causal_depthwise_conv(u, conv_dw_w[l], conv_dw_b[l])
        c = jax.nn.silu(layer_norm(c, conv_ln_gain[l], conv_ln_bias[l]))
        c = c @ w_conv_out[l] + b_conv_out[l]

        h_res = h_res + (gate_attn * a + gate_conv * c) @ w_out[l]

        hn = rms_norm(h_res, norm_mlp_gain[l])
        h_res = h_res + jnp.square(jax.nn.relu(hn @ w_mlp_up[l])) @ w_mlp_down[l]
    y = rms_norm(h_res, final_norm_gain)
    return y[:, N_META:]
```

```python
import functools

import jax
import jax.numpy as jnp
from jax import lax
from jax.experimental import pallas as pl
from jax.experimental.pallas import tpu as pltpu

D_MODEL = 1024
N_META = 16
N_HEADS = 16
HEAD_DIM = 64
D_ATTN = N_HEADS * HEAD_DIM
D_CONV = D_MODEL
CONV_WIDTH = 31
D_FF = 4 * D_MODEL
RMS_EPS = 1e-6
LN_EPS = 1e-5

LANES = 128
HEADS_PER_BLOCK = LANES // HEAD_DIM
N_HEAD_BLOCKS = N_HEADS // HEADS_PER_BLOCK
VMEM_LIMIT = 56 * 1024 * 1024

BF16 = jnp.bfloat16
F32 = jnp.float32


def _const_spec(shape):
    nd = len(shape)
    return pl.BlockSpec(shape, lambda *_: (0,) * nd, pipeline_mode=pl.Buffered(1))


def _sigmoid(x):
    return 1.0 / (1.0 + jnp.exp(-x))


def _rms_norm(x, g):
    ms = jnp.mean(x * x, axis=-1, keepdims=True)
    return x * lax.rsqrt(ms + RMS_EPS) * g


def _in_proj_kernel(x_ref, g_ref, wqkv_ref, wfgt_ref, bf_ref, wglu_ref, bglu_ref,
                    q_ref, k_ref, v_ref, fgt_ref, u_ref):
    hn = _rms_norm(x_ref[...], g_ref[...]).astype(BF16)
    scale = HEAD_DIM ** -0.5
    q = jnp.dot(hn, wqkv_ref[:, 0:D_ATTN], preferred_element_type=F32)
    q_ref[...] = (q * scale).astype(BF16)
    k = jnp.dot(hn, wqkv_ref[:, D_ATTN:2 * D_ATTN], preferred_element_type=F32)
    k_ref[...] = k.astype(BF16)
    v = jnp.dot(hn, wqkv_ref[:, 2 * D_ATTN:3 * D_ATTN], preferred_element_type=F32)
    v_ref[...] = v.astype(BF16)
    fgt = lax.dot_general(wfgt_ref[...], hn, (((1,), (1,)), ((), ())), preferred_element_type=F32)
    fgt_ref[...] = fgt + bf_ref[...]
    a = jnp.dot(hn, wglu_ref[:, 0:D_CONV], preferred_element_type=F32) + bglu_ref[:, 0:D_CONV]
    gl = jnp.dot(hn, wglu_ref[:, D_CONV:2 * D_CONV], preferred_element_type=F32) + bglu_ref[:, D_CONV:2 * D_CONV]
    u_ref[...] = a * _sigmoid(gl)


def _in_proj(x, g, wqkv, wfgt, bf, wglu, bglu, tm):
    b, s, d = x.shape
    grid = (b, s // tm)
    row = lambda w: pl.BlockSpec((None, tm, w), lambda i, j: (i, j, 0))
    return pl.pallas_call(
        _in_proj_kernel,
        grid=grid,
        in_specs=[row(d), _const_spec(g.shape), _const_spec(wqkv.shape), _const_spec(wfgt.shape),
                  _const_spec(bf.shape), _const_spec(wglu.shape), _const_spec(bglu.shape)],
        out_specs=[row(D_ATTN), row(D_ATTN), row(D_ATTN),
                   pl.BlockSpec((None, N_HEADS, tm), lambda i, j: (i, 0, j)),
                   row(D_CONV)],
        out_shape=[jax.ShapeDtypeStruct((b, s, D_ATTN), BF16)] * 3
        + [jax.ShapeDtypeStruct((b, N_HEADS, s), F32), jax.ShapeDtypeStruct((b, s, D_CONV), F32)],
        compiler_params=pltpu.CompilerParams(dimension_semantics=("arbitrary", "arbitrary"),
                                             vmem_limit_bytes=VMEM_LIMIT),
        name="in_proj",
    )(x, g, wqkv, wfgt, bf, wglu, bglu)


def _log_sigmoid(x):
    return jnp.minimum(x, 0.0) - jnp.log(1.0 + jnp.exp(-jnp.abs(x)))


def _cumsum_lanes(x, tri):
    hi = x.astype(BF16)
    r1 = x - hi.astype(F32)
    mid = r1.astype(BF16)
    lo = (r1 - mid.astype(F32)).astype(BF16)
    dot = lambda t: jnp.dot(t, tri, preferred_element_type=F32)
    return dot(hi) + dot(mid) + dot(lo)


def _tri(n):
    r = lax.broadcasted_iota(jnp.int32, (n, n), 0)
    c = lax.broadcasted_iota(jnp.int32, (n, n), 1)
    return jnp.where(r <= c, 1.0, 0.0).astype(BF16)


DECAY_CHUNK = 256


def _decay_kernel(fgm_ref, fg_ref, ncm_ref, nc_ref):
    cum_m = _cumsum_lanes(_log_sigmoid(fgm_ref[...]), _tri(N_META))
    ncm_ref[...] = -cum_m
    carry = cum_m[:, N_META - 1:N_META]
    tri = _tri(DECAY_CHUNK)
    for c in range(fg_ref.shape[-1] // DECAY_CHUNK):
        sl = slice(c * DECAY_CHUNK, (c + 1) * DECAY_CHUNK)
        cum = _cumsum_lanes(_log_sigmoid(fg_ref[:, sl]), tri) + carry
        nc_ref[:, sl] = -cum
        carry = cum[:, DECAY_CHUNK - 1:DECAY_CHUNK]


def _decay(fgt_meta, fgt):
    b, h, s = fgt.shape
    return pl.pallas_call(
        _decay_kernel,
        grid=(b,),
        in_specs=[_const_spec(fgt_meta.shape), pl.BlockSpec((None, h, s), lambda i: (i, 0, 0))],
        out_specs=[pl.BlockSpec((None, h, N_META), lambda i: (i, 0, 0)),
                   pl.BlockSpec((None, h, s), lambda i: (i, 0, 0))],
        out_shape=[jax.ShapeDtypeStruct((b, h, N_META), F32), jax.ShapeDtypeStruct((b, h, s), F32)],
        compiler_params=pltpu.CompilerParams(dimension_semantics=("arbitrary",)),
        name="decay",
    )(fgt_meta, fgt)


def _attn_kernel(q_ref, k_ref, v_ref, km_ref, vm_ref, nc_ref, ncm_ref, o_ref, *, tq, tk):
    qi = pl.program_id(2)
    q = q_ref[...]
    lane = lax.broadcasted_iota(jnp.int32, q.shape, 1)
    nt = (((1,), (1,)), ((), ()))
    outs = []
    for h in range(HEADS_PER_BLOCK):
        in_head = (lane >= h * HEAD_DIM) & (lane < (h + 1) * HEAD_DIM)
        qh = jnp.where(in_head, q, jnp.zeros_like(q))

        s = lax.dot_general(qh, km_ref[...], nt, preferred_element_type=F32) + ncm_ref[h:h + 1, :]
        m = jnp.max(s, axis=-1, keepdims=True)
        p = jnp.exp(s - m)
        l = jnp.sum(p, axis=-1, keepdims=True)
        acc = jnp.dot(p.astype(BF16), vm_ref[...], preferred_element_type=F32)

        def block(j, carry, masked):
            m, l, acc = carry
            start = pl.multiple_of(j * tk, tk)
            kb = k_ref[pl.ds(start, tk), :]
            vb = v_ref[pl.ds(start, tk), :]
            s = lax.dot_general(qh, kb, nt, preferred_element_type=F32) + nc_ref[h:h + 1, pl.ds(start, tk)]
            if masked:
                r = lax.broadcasted_iota(jnp.int32, s.shape, 0)
                c = lax.broadcasted_iota(jnp.int32, s.shape, 1)
                s = jnp.where(r >= c, s, -jnp.inf)
            m_new = jnp.maximum(m, jnp.max(s, axis=-1, keepdims=True))
            alpha = jnp.exp(m - m_new)
            p = jnp.exp(s - m_new)
            l = alpha * l + jnp.sum(p, axis=-1, keepdims=True)
            acc = alpha * acc + jnp.dot(p.astype(BF16), vb, preferred_element_type=F32)
            return m_new, l, acc

        m, l, acc = lax.fori_loop(0, qi, functools.partial(block, masked=False), (m, l, acc))
        m, l, acc = block(qi, (m, l, acc), True)
        outs.append(acc / l)
    o = jnp.where(lane < HEAD_DIM, outs[0], outs[1])
    o_ref[...] = o.astype(o_ref.dtype)


def _attention(q, k, v, k_meta, v_meta, nc, nc_meta, tq):
    b, s, _ = q.shape
    tk = tq
    nc4 = nc.reshape(b, N_HEAD_BLOCKS, HEADS_PER_BLOCK, s)
    ncm4 = nc_meta.reshape(b, N_HEAD_BLOCKS, HEADS_PER_BLOCK, N_META)
    grid = (b, N_HEAD_BLOCKS, s // tq)
    return pl.pallas_call(
        functools.partial(_attn_kernel, tq=tq, tk=tk),
        grid=grid,
        in_specs=[
            pl.BlockSpec((None, tq, LANES), lambda i, hb, j: (i, j, hb)),
            pl.BlockSpec((None, s, LANES), lambda i, hb, j: (i, 0, hb)),
            pl.BlockSpec((None, s, LANES), lambda i, hb, j: (i, 0, hb)),
            pl.BlockSpec((N_META, LANES), lambda i, hb, j: (0, hb)),
            pl.BlockSpec((N_META, LANES), lambda i, hb, j: (0, hb)),
            pl.BlockSpec((None, None, HEADS_PER_BLOCK, s), lambda i, hb, j: (i, hb, 0, 0)),
            pl.BlockSpec((None, None, HEADS_PER_BLOCK, N_META), lambda i, hb, j: (i, hb, 0, 0)),
        ],
        out_specs=pl.BlockSpec((None, tq, LANES), lambda i, hb, j: (i, j, hb)),
        out_shape=jax.ShapeDtypeStruct((b, s, D_ATTN), BF16),
        compiler_params=pltpu.CompilerParams(dimension_semantics=("arbitrary",) * 3,
                                             vmem_limit_bytes=VMEM_LIMIT),
        name="fox_attention",
    )(q, k, v, k_meta, v_meta, nc4, ncm4)


CONV_HALO = 32


def _conv_kernel(u_ref, halo_ref, mhalo_ref, w_ref, b_ref, lg_ref, lb_ref, o_ref, win_ref, y_ref, *, tt):
    first = pl.program_id(1) == 0
    halo = jnp.where(first, mhalo_ref[...], halo_ref[...])
    win_ref[0:CONV_HALO, :] = halo
    win_ref[CONV_HALO:CONV_HALO + tt, :] = u_ref[...]
    off = CONV_HALO - (CONV_WIDTH - 1)
    for c in range(D_CONV // LANES):
        cs = slice(c * LANES, (c + 1) * LANES)
        acc = jnp.zeros((tt, LANES), F32) + b_ref[:, cs]
        for j in range(CONV_WIDTH):
            acc = acc + win_ref[off + j:off + j + tt, cs] * w_ref[j:j + 1, cs]
        y_ref[:, cs] = acc
    y = y_ref[...]
    mu = jnp.mean(y, axis=-1, keepdims=True)
    yc = y - mu
    var = jnp.mean(yc * yc, axis=-1, keepdims=True)
    z = yc * lax.rsqrt(var + LN_EPS) * lg_ref[...] + lb_ref[...]
    o_ref[...] = (z * _sigmoid(z)).astype(o_ref.dtype)


def _conv(u, meta_halo, w, bias, ln_g, ln_b, tt):
    b, s, c = u.shape
    hb = tt // CONV_HALO
    return pl.pallas_call(
        functools.partial(_conv_kernel, tt=tt),
        grid=(b, s // tt),
        in_specs=[
            pl.BlockSpec((None, tt, c), lambda i, j: (i, j, 0)),
            pl.BlockSpec((None, CONV_HALO, c), lambda i, j: (i, jnp.maximum(j * hb - 1, 0), 0)),
            _const_spec(meta_halo.shape), _const_spec(w.shape), _const_spec(bias.shape),
            _const_spec(ln_g.shape), _const_spec(ln_b.shape),
        ],
        out_specs=pl.BlockSpec((None, tt, c), lambda i, j: (i, j, 0)),
        out_shape=jax.ShapeDtypeStruct((b, s, c), BF16),
        scratch_shapes=[pltpu.VMEM((CONV_HALO + tt, c), F32), pltpu.VMEM((tt, c), F32)],
        compiler_params=pltpu.CompilerParams(dimension_semantics=("arbitrary", "arbitrary"),
                                             vmem_limit_bytes=VMEM_LIMIT),
        name="conv_ln_swish",
    )(u, u, meta_halo, w, bias, ln_g, ln_b)


def _merge_kernel(x_ref, at_ref, cv_ref, g_ref, wg_ref, wao_ref, wco_ref, bco_ref, wout_ref, o_ref):
    x = x_ref[...]
    hn = _rms_norm(x, g_ref[...]).astype(BF16)
    ga = _sigmoid(jnp.dot(hn, wg_ref[:, 0:D_MODEL], preferred_element_type=F32))
    gc = _sigmoid(jnp.dot(hn, wg_ref[:, D_MODEL:2 * D_MODEL], preferred_element_type=F32))
    a = jnp.dot(at_ref[...], wao_ref[...], preferred_element_type=F32)
    c = jnp.dot(cv_ref[...], wco_ref[...], preferred_element_type=F32) + bco_ref[...]
    mix = (ga * a + gc * c).astype(BF16)
    o_ref[...] = x + jnp.dot(mix, wout_ref[...], preferred_element_type=F32)


def _merge(x, attn, cact, g, wg, wao, wco, bco, wout, tm):
    b, s, d = x.shape
    row = lambda: pl.BlockSpec((None, tm, d), lambda i, j: (i, j, 0))
    return pl.pallas_call(
        _merge_kernel,
        grid=(b, s // tm),
        in_specs=[row(), row(), row(), _const_spec(g.shape), _const_spec(wg.shape), _const_spec(wao.shape),
                  _const_spec(wco.shape), _const_spec(bco.shape), _const_spec(wout.shape)],
        out_specs=row(),
        out_shape=jax.ShapeDtypeStruct((b, s, d), F32),
        compiler_params=pltpu.CompilerParams(dimension_semantics=("arbitrary", "arbitrary"),
                                             vmem_limit_bytes=VMEM_LIMIT),
        name="merge",
    )(x, attn, cact, g, wg, wao, wco, bco, wout)


def _mlp_kernel(h_ref, g_ref, wup_ref, wdn_ref, gf_ref, o_ref):
    h = h_ref[...]
    hn = _rms_norm(h, g_ref[...]).astype(BF16)
    up = jnp.dot(hn, wup_ref[...], preferred_element_type=F32)
    act = jnp.square(jnp.maximum(up, 0.0)).astype(BF16)
    h2 = h + jnp.dot(act, wdn_ref[...], preferred_element_type=F32)
    o_ref[...] = _rms_norm(h2, gf_ref[...])


def _mlp(h, g, wup, wdn, gf, tm):
    b, s, d = h.shape
    row = lambda: pl.BlockSpec((None, tm, d), lambda i, j: (i, j, 0))
    return pl.pallas_call(
        _mlp_kernel,
        grid=(b, s // tm),
        in_specs=[row(), _const_spec(g.shape), _const_spec(wup.shape), _const_spec(wdn.shape),
                  _const_spec(gf.shape)],
        out_specs=row(),
        out_shape=jax.ShapeDtypeStruct((b, s, d), F32),
        compiler_params=pltpu.CompilerParams(dimension_semantics=("arbitrary", "arbitrary"),
                                             vmem_limit_bytes=VMEM_LIMIT),
        name="mlp_final_norm",
    )(h, g, wup, wdn, gf)


def kernel(x, meta_tokens, norm_mix_gain, w_in, b_forget, w_attn_out, b_glu, conv_dw_w, conv_dw_b,
           conv_ln_gain, conv_ln_bias, w_conv_out, b_conv_out, w_out, norm_mlp_gain, w_mlp_up,
           w_mlp_down, final_norm_gain):
    assert w_in.shape[0] == 1, "single layer"
    o_qkv, o_fg, o_glu, o_gate = 0, 3 * D_ATTN, 3 * D_ATTN + N_HEADS, 3 * D_ATTN + N_HEADS + 2 * D_CONV
    w = w_in[0]
    wqkv = w[:, o_qkv:o_fg].astype(BF16)
    wfgt = w[:, o_fg:o_glu].T.astype(BF16)
    wglu = w[:, o_glu:o_gate].astype(BF16)
    wgate = w[:, o_gate:].astype(BF16)
    g_mix = norm_mix_gain[0][None, :]
    bf = b_forget[0][:, None]
    bglu = b_glu[0][None, :]

    proj = functools.partial(_in_proj, g=g_mix, wqkv=wqkv, wfgt=wfgt, bf=bf, wglu=wglu, bglu=bglu)
    _, k_m, v_m, fgt_m, u_m = proj(meta_tokens[None].astype(x.dtype), tm=N_META)
    q, k, v, fgt, u = proj(x, tm=512)

    nc_meta, nc = _decay(fgt_m[0], fgt)
    attn = _attention(q, k, v, k_m[0], v_m[0], nc, nc_meta, tq=256)

    meta_halo = jnp.concatenate([jnp.zeros((CONV_HALO - N_META, D_CONV), F32), u_m[0]], axis=0)
    cact = _conv(u, meta_halo, conv_dw_w[0], conv_dw_b[0][None, :], conv_ln_gain[0][None, :],
                 conv_ln_bias[0][None, :], tt=256)

    h1 = _merge(x, attn, cact, g_mix, wgate, w_attn_out[0].astype(BF16), w_conv_out[0].astype(BF16),
                b_conv_out[0][None, :], w_out[0].astype(BF16), tm=512)
    return _mlp(h1, norm_mlp_gain[0][None, :], w_mlp_up[0].astype(BF16), w_mlp_down[0].astype(BF16),
                final_norm_gain[None, :], tm=512)
```

```python
import functools

import jax
import jax.numpy as jnp
from jax import lax
from jax.experimental import pallas as pl
from jax.experimental.pallas import tpu as pltpu

D_MODEL = 1024
N_META = 16
N_HEADS = 16
HEAD_DIM = 64
D_ATTN = N_HEADS * HEAD_DIM
D_CONV = D_MODEL
CONV_WIDTH = 31
D_FF = 4 * D_MODEL
RMS_EPS = 1e-6
LN_EPS = 1e-5
LOG2E = 1.4426950408889634

LANES = 128
SUBLANES = 8
BF16_ROWS = 16
HEADS_PER_BLOCK = LANES // HEAD_DIM
N_HEAD_BLOCKS = N_HEADS // HEADS_PER_BLOCK
N_SPLIT = 3
CONV_HALO = 32
CONV_ROWS = 128
VMEM_LIMIT = 56 * 1024 * 1024

BF16 = jnp.bfloat16
F32 = jnp.float32
NT_DIMS = (((1,), (1,)), ((), ()))


def _const_spec(shape):
    nd = len(shape)
    return pl.BlockSpec(shape, lambda *_: (0,) * nd, pipeline_mode=pl.Buffered(1))


def _sigmoid(x):
    return 1.0 / (1.0 + jnp.exp(-x))


def _rms_norm(x, g):
    ms = jnp.mean(x * x, axis=-1, keepdims=True)
    return x * lax.rsqrt(ms + RMS_EPS) * g


def _log_sigmoid(x):
    return jnp.minimum(x, 0.0) - jnp.log(1.0 + jnp.exp(-jnp.abs(x)))


def _split3_by_lane_group(x, lane):
    hi = x.astype(BF16).astype(F32)
    r1 = x - hi
    mid = r1.astype(BF16).astype(F32)
    lo = (r1 - mid).astype(BF16).astype(F32)
    out = jnp.where(lane < N_HEADS, hi,
                    jnp.where(lane < 2 * N_HEADS, mid, jnp.where(lane < 3 * N_HEADS, lo, 0.0)))
    return out.astype(BF16)


def _in_proj_kernel(x_ref, g_ref, wqk_ref, wvt_ref, wfg_ref, bfg_ref, wglu_ref, bglu_ref, tril_ref, carry0_ref,
                    q_ref, k_ref, vt_ref, ncp_ref, u_ref, carry_out_ref, carry_ref):
    tm = x_ref.shape[0]
    hn = _rms_norm(x_ref[...], g_ref[...]).astype(BF16)
    scale = HEAD_DIM ** -0.5 * LOG2E
    q = jnp.dot(hn, wqk_ref[:, 0:D_ATTN], preferred_element_type=F32)
    q_ref[...] = (q * scale).astype(BF16)
    k = jnp.dot(hn, wqk_ref[:, D_ATTN:2 * D_ATTN], preferred_element_type=F32)
    k_ref[...] = k.astype(BF16)
    vt = lax.dot_general(wvt_ref[...], hn, NT_DIMS, preferred_element_type=F32)
    vt_ref[...] = vt.astype(BF16)

    a = jnp.dot(hn, wglu_ref[:, 0:D_CONV], preferred_element_type=F32) + bglu_ref[:, 0:D_CONV]
    gl = jnp.dot(hn, wglu_ref[:, D_CONV:2 * D_CONV], preferred_element_type=F32) + bglu_ref[:, D_CONV:2 * D_CONV]
    u_ref[...] = a * _sigmoid(gl)

    lane = lax.broadcasted_iota(jnp.int32, (tm, LANES), 1)
    fg = jnp.dot(hn, wfg_ref[...], preferred_element_type=F32) + bfg_ref[...]
    parts = _split3_by_lane_group(_log_sigmoid(fg), lane)
    cum3 = jnp.dot(tril_ref[...], parts, preferred_element_type=F32)
    tot = cum3 + pltpu.roll(cum3, LANES - N_HEADS, 1) + pltpu.roll(cum3, LANES - 2 * N_HEADS, 1)
    tot3 = jnp.where(lane < N_HEADS, tot,
                     jnp.where(lane < 2 * N_HEADS, pltpu.roll(tot, N_HEADS, 1), pltpu.roll(tot, 2 * N_HEADS, 1)))

    @pl.when(pl.program_id(1) == 0)
    def _():
        carry_ref[...] = carry0_ref[...]

    cum = tot3 + carry_ref[...]
    last = cum[tm - 1:tm, :]
    carry_ref[...] = last
    carry_out_ref[...] = last
    ncp_ref[...] = _split3_by_lane_group(-LOG2E * cum, lane)


def _in_proj(x, carry0, g, wqk, wvt, wfg, bfg, wglu, bglu, tm):
    b, s, d = x.shape
    tril = jnp.tril(jnp.ones((tm, tm), F32)).astype(BF16)
    row = lambda w: pl.BlockSpec((None, tm, w), lambda i, j: (i, j, 0))
    consts = (g, wqk, wvt, wfg, bfg, wglu, bglu, tril, carry0)
    return pl.pallas_call(
        _in_proj_kernel,
        grid=(b, s // tm),
        in_specs=[row(d)] + [_const_spec(c.shape) for c in consts],
        out_specs=[row(D_ATTN), row(D_ATTN),
                   pl.BlockSpec((None, D_ATTN, tm), lambda i, j: (i, 0, j)),
                   row(LANES), row(D_CONV),
                   pl.BlockSpec((None, 1, LANES), lambda i, j: (i, 0, 0))],
        out_shape=[jax.ShapeDtypeStruct((b, s, D_ATTN), BF16), jax.ShapeDtypeStruct((b, s, D_ATTN), BF16),
                   jax.ShapeDtypeStruct((b, D_ATTN, s), BF16), jax.ShapeDtypeStruct((b, s, LANES), BF16),
                   jax.ShapeDtypeStruct((b, s, D_CONV), F32), jax.ShapeDtypeStruct((b, 1, LANES), F32)],
        scratch_shapes=[pltpu.VMEM((1, LANES), F32)],
        compiler_params=pltpu.CompilerParams(dimension_semantics=("arbitrary", "arbitrary"),
                                             vmem_limit_bytes=VMEM_LIMIT),
        name="in_proj",
    )(x, *consts)


def _attn_kernel(q_ref, k_ref, ncp_ref, vt_ref, km_ref, ncpm_ref, vtm_ref, o_ref, *, tq):
    hb = pl.program_id(1)
    seq = q_ref.shape[0]
    lane = lax.broadcasted_iota(jnp.int32, (tq, LANES), 1)
    key_row = lax.broadcasted_iota(jnp.int32, (tq, tq), 0)
    qry_col = lax.broadcasted_iota(jnp.int32, (tq, tq), 1)
    causal = key_row <= qry_col
    ones_m = jnp.ones((BF16_ROWS, N_META), BF16)
    kaug_m = jnp.concatenate([km_ref[...], ncpm_ref[...]], axis=1)
    head_masks, sels, vtm_aug = [], [], []
    for h in range(HEADS_PER_BLOCK):
        head_masks.append((lane >= h * HEAD_DIM) & (lane < (h + 1) * HEAD_DIM))
        hg = hb * HEADS_PER_BLOCK + h
        pick = (lane == hg) | (lane == hg + N_HEADS) | (lane == hg + 2 * N_HEADS)
        sels.append(jnp.where(pick, 1.0, 0.0).astype(BF16))
        vtm_aug.append(jnp.concatenate([vtm_ref[h * HEAD_DIM:(h + 1) * HEAD_DIM, :], ones_m], axis=0))

    def scores(qi):
        nk = (qi + 1) * tq
        qb = q_ref[qi * tq:(qi + 1) * tq, :]
        qaug = jnp.concatenate(
            [jnp.concatenate([jnp.where(head_masks[h], qb, jnp.zeros_like(qb)), sels[h]], axis=1)
             for h in range(HEADS_PER_BLOCK)], axis=0)
        kaug = jnp.concatenate(
            [jnp.concatenate([k_ref[0:nk, :], ncp_ref[0:nk, :]], axis=1), kaug_m], axis=0)
        return lax.dot_general(kaug, qaug, NT_DIMS, preferred_element_type=F32)

    def finish(qi, s_all):
        nk = (qi + 1) * tq
        ones_k = jnp.ones((BF16_ROWS, nk), BF16)
        outs = []
        for h in range(HEADS_PER_BLOCK):
            s = s_all[:, h * tq:(h + 1) * tq]
            pieces = [jnp.where(causal, s[nk - tq:nk], -jnp.inf), s[nk:nk + N_META]]
            if qi > 0:
                pieces.insert(0, s[0:nk - tq])
            m = functools.reduce(jnp.maximum, [jnp.max(x, axis=0, keepdims=True) for x in pieces])
            p = [jnp.exp2(x - m).astype(BF16) for x in pieces]
            vta = jnp.concatenate([vt_ref[h * HEAD_DIM:(h + 1) * HEAD_DIM, 0:nk], ones_k], axis=0)
            acc = (jnp.dot(vta, jnp.concatenate(p[:-1], axis=0), preferred_element_type=F32)
                   + jnp.dot(vtm_aug[h], p[-1], preferred_element_type=F32))
            outs.append(acc[0:HEAD_DIM] * (1.0 / acc[HEAD_DIM:HEAD_DIM + 1]))
        o_ref[qi * tq:(qi + 1) * tq, :] = jnp.concatenate(outs, axis=0).T.astype(o_ref.dtype)

    nq = seq // tq
    s_next = scores(0)
    for qi in range(nq):
        s_cur = s_next
        if qi + 1 < nq:
            s_next = scores(qi + 1)
        finish(qi, s_cur)


def _attention(q, k, ncp, vt, k_meta, ncp_meta, vt_meta, tq):
    b, s, _ = q.shape
    seq = lambda: pl.BlockSpec((None, s, LANES), lambda i, hb: (i, 0, hb))
    return pl.pallas_call(
        functools.partial(_attn_kernel, tq=tq),
        grid=(b, N_HEAD_BLOCKS),
        in_specs=[
            seq(), seq(),
            pl.BlockSpec((None, s, LANES), lambda i, hb: (i, 0, 0)),
            pl.BlockSpec((None, LANES, s), lambda i, hb: (i, hb, 0)),
            pl.BlockSpec((N_META, LANES), lambda i, hb: (0, hb)),
            pl.BlockSpec((N_META, LANES), lambda i, hb: (0, 0)),
            pl.BlockSpec((LANES, N_META), lambda i, hb: (hb, 0)),
        ],
        out_specs=seq(),
        out_shape=jax.ShapeDtypeStruct((b, s, D_ATTN), BF16),
        compiler_params=pltpu.CompilerParams(dimension_semantics=("arbitrary", "arbitrary"),
                                             vmem_limit_bytes=VMEM_LIMIT),
        name="fox_attention",
    )(q, k, ncp, vt, k_meta, ncp_meta, vt_meta)


def _depthwise_conv(win_ref, w_ref, b_ref, y_ref, rows_total):
    off = CONV_HALO - (CONV_WIDTH - 1)
    for r0 in range(0, rows_total, CONV_ROWS):
        for c in range(D_CONV // LANES):
            cs = slice(c * LANES, (c + 1) * LANES)
            win = win_ref[r0:r0 + CONV_ROWS + CONV_HALO, cs]
            acc = jnp.zeros((CONV_ROWS, LANES), F32) + b_ref[:, cs]
            nwin = win.shape[0]
            for shift in range(SUBLANES):
                taps = [j for j in range(CONV_WIDTH) if (off + j) % SUBLANES == shift]
                shifted = pltpu.roll(win, nwin - shift, 0) if shift else win
                for j in taps:
                    base = (off + j) // SUBLANES * SUBLANES
                    acc = acc + shifted[base:base + CONV_ROWS] * w_ref[j:j + 1, cs]
            y_ref[r0:r0 + CONV_ROWS, cs] = acc


def _back_kernel(x_ref, at_ref, u_ref, halo_ref, mhalo_ref, cw_ref, cb_ref, lg_ref, lb_ref, g_ref, wg_ref,
                 wao_ref, wco_ref, bco_ref, wout_ref, g2_ref, wup_ref, wdn_ref, gf_ref, o_ref,
                 win_ref, y_ref, cact_ref, *, tiles_per_row, n_tiles):
    tm = x_ref.shape[0]
    t = pl.program_id(0)

    @pl.when(t == 0)
    def _():
        cact_ref[...] = jnp.zeros_like(cact_ref)

    row_start = lax.rem(jnp.minimum(t, n_tiles - 1), tiles_per_row) == 0
    win_ref[0:CONV_HALO, :] = jnp.where(row_start, mhalo_ref[...], halo_ref[...])
    win_ref[CONV_HALO:CONV_HALO + tm, :] = u_ref[...]
    _depthwise_conv(win_ref, cw_ref, cb_ref, y_ref, tm)
    y = y_ref[...]
    mu = jnp.mean(y, axis=-1, keepdims=True)
    yc = y - mu
    var = jnp.mean(yc * yc, axis=-1, keepdims=True)
    z = yc * lax.rsqrt(var + LN_EPS) * lg_ref[...] + lb_ref[...]
    slot = lax.rem(t, 2)
    cact_prev = cact_ref[1 - slot]
    cact_ref[slot] = (z * _sigmoid(z)).astype(BF16)

    x = x_ref[...]
    hn = _rms_norm(x, g_ref[...]).astype(BF16)
    ga = _sigmoid(jnp.dot(hn, wg_ref[:, 0:D_MODEL], preferred_element_type=F32))
    gc = _sigmoid(jnp.dot(hn, wg_ref[:, D_MODEL:2 * D_MODEL], preferred_element_type=F32))
    a = jnp.dot(at_ref[...], wao_ref[...], preferred_element_type=F32)
    c = jnp.dot(cact_prev, wco_ref[...], preferred_element_type=F32) + bco_ref[...]
    mix = (ga * a + gc * c).astype(BF16)
    h1 = x + jnp.dot(mix, wout_ref[...], preferred_element_type=F32)
    hn2 = _rms_norm(h1, g2_ref[...]).astype(BF16)
    up = jnp.dot(hn2, wup_ref[...], preferred_element_type=F32)
    act = jnp.square(jnp.maximum(up, 0.0)).astype(BF16)
    h2 = h1 + jnp.dot(act, wdn_ref[...], preferred_element_type=F32)
    o_ref[...] = _rms_norm(h2, gf_ref[...])


def _back(x, attn, u, meta_halo, consts, tm):
    b, s, d = x.shape
    nj = s // tm
    n_tiles = b * nj
    halo_blocks = tm // CONV_HALO
    prev = lambda t: jnp.maximum(t - 1, 0)
    cur = lambda t: jnp.minimum(t, n_tiles - 1)
    chain = lambda: pl.BlockSpec((None, tm, d), lambda t: (prev(t) // nj, prev(t) % nj, 0))
    return pl.pallas_call(
        functools.partial(_back_kernel, tiles_per_row=nj, n_tiles=n_tiles),
        grid=(n_tiles + 1,),
        in_specs=[chain(), chain(),
                  pl.BlockSpec((None, tm, d), lambda t: (cur(t) // nj, cur(t) % nj, 0)),
                  pl.BlockSpec((None, CONV_HALO, d),
                               lambda t: (cur(t) // nj, jnp.maximum(cur(t) % nj * halo_blocks - 1, 0), 0)),
                  _const_spec(meta_halo.shape)] + [_const_spec(c.shape) for c in consts],
        out_specs=chain(),
        out_shape=jax.ShapeDtypeStruct((b, s, d), F32),
        scratch_shapes=[pltpu.VMEM((CONV_HALO + tm, d), F32), pltpu.VMEM((tm, d), F32),
                        pltpu.VMEM((2, tm, d), BF16)],
        compiler_params=pltpu.CompilerParams(dimension_semantics=("arbitrary",),
                                             vmem_limit_bytes=VMEM_LIMIT),
        name="conv_merge_mlp",
    )(x, attn, u, u, meta_halo, *consts)


def kernel(x, meta_tokens, norm_mix_gain, w_in, b_forget, w_attn_out, b_glu, conv_dw_w, conv_dw_b,
           conv_ln_gain, conv_ln_bias, w_conv_out, b_conv_out, w_out, norm_mlp_gain, w_mlp_up,
           w_mlp_down, final_norm_gain):
    assert w_in.shape[0] == 1, "single layer"
    o_v, o_fg, o_glu = 2 * D_ATTN, 3 * D_ATTN, 3 * D_ATTN + N_HEADS
    o_gate = o_glu + 2 * D_CONV
    w = w_in[0]
    wqk = w[:, 0:o_v].astype(BF16)
    wvt = w[:, o_v:o_fg].T.astype(BF16)
    lane_pad = LANES - N_SPLIT * N_HEADS
    wfg = jnp.pad(jnp.tile(w[:, o_fg:o_glu], (1, N_SPLIT)), ((0, 0), (0, lane_pad))).astype(BF16)
    bfg = jnp.pad(jnp.tile(b_forget[0], N_SPLIT), (0, lane_pad))[None, :]
    wglu = w[:, o_glu:o_gate].astype(BF16)
    wgate = w[:, o_gate:].astype(BF16)
    g_mix = norm_mix_gain[0][None, :]
    bglu = b_glu[0][None, :]

    proj = functools.partial(_in_proj, g=g_mix, wqk=wqk, wvt=wvt, wfg=wfg, bfg=bfg, wglu=wglu, bglu=bglu)
    _, k_m, vt_m, ncp_m, u_m, carry_m = proj(meta_tokens[None].astype(x.dtype), jnp.zeros((1, LANES), F32),
                                              tm=N_META)
    q, k, vt, ncp, u, _ = proj(x, carry_m[0], tm=512)

    attn = _attention(q, k, ncp, vt, k_m[0], ncp_m[0], vt_m[0], tq=256)

    meta_halo = jnp.concatenate([jnp.zeros((CONV_HALO - N_META, D_CONV), F32), u_m[0]], axis=0)
    vec = lambda v: v[None, :]
    consts = (conv_dw_w[0], vec(conv_dw_b[0]), vec(conv_ln_gain[0]), vec(conv_ln_bias[0]), g_mix, wgate,
              w_attn_out[0].astype(BF16), w_conv_out[0].astype(BF16), vec(b_conv_out[0]), w_out[0].astype(BF16),
              vec(norm_mlp_gain[0]), w_mlp_up[0].astype(BF16), w_mlp_down[0].astype(BF16), vec(final_norm_gain))
    return _back(x, attn, u, meta_halo, consts, tm=256)
```

```python
import functools

import jax
import jax.numpy as jnp
from jax import lax
from jax.experimental import pallas as pl
from jax.experimental.pallas import tpu as pltpu

D_MODEL = 1024
N_META = 16
N_HEADS = 16
HEAD_DIM = 64
D_ATTN = N_HEADS * HEAD_DIM
D_CONV = D_MODEL
CONV_WIDTH = 31
D_FF = 4 * D_MODEL
RMS_EPS = 1e-6
LN_EPS = 1e-5
LOG2E = 1.4426950408889634

LANES = 128
SUBLANES = 8
BF16_ROWS = 16
HEADS_PER_BLOCK = LANES // HEAD_DIM
N_HEAD_BLOCKS = N_HEADS // HEADS_PER_BLOCK
N_SPLIT = 3
CONV_HALO = 32
CONV_ROWS = 128
VMEM_LIMIT = 56 * 1024 * 1024

BF16 = jnp.bfloat16
F32 = jnp.float32
NT_DIMS = (((1,), (1,)), ((), ()))


def _const_spec(shape):
    nd = len(shape)
    return pl.BlockSpec(shape, lambda *_: (0,) * nd, pipeline_mode=pl.Buffered(1))


def _sigmoid(x):
    return 1.0 / (1.0 + jnp.exp(-x))


def _rms_norm(x, g):
    ms = jnp.mean(x * x, axis=-1, keepdims=True)
    return x * lax.rsqrt(ms + RMS_EPS) * g


def _log_sigmoid(x):
    return jnp.minimum(x, 0.0) - jnp.log(1.0 + jnp.exp(-jnp.abs(x)))


def _split3_by_lane_group(x, lane):
    hi = x.astype(BF16).astype(F32)
    r1 = x - hi
    mid = r1.astype(BF16).astype(F32)
    lo = (r1 - mid).astype(BF16).astype(F32)
    out = jnp.where(lane < N_HEADS, hi,
                    jnp.where(lane < 2 * N_HEADS, mid, jnp.where(lane < 3 * N_HEADS, lo, 0.0)))
    return out.astype(BF16)


def _in_proj_kernel(x_ref, g_ref, wqk_ref, wvt_ref, wfg_ref, bfg_ref, wglu_ref, bglu_ref, tril_ref, carry0_ref,
                    q_ref, k_ref, vt_ref, ncp_ref, u_ref, carry_out_ref, carry_ref):
    tm = x_ref.shape[0]
    hn = _rms_norm(x_ref[...], g_ref[...]).astype(BF16)
    lane = lax.broadcasted_iota(jnp.int32, (tm, LANES), 1)
    fg = jnp.dot(hn, wfg_ref[...], preferred_element_type=F32) + bfg_ref[...]
    parts = _split3_by_lane_group(_log_sigmoid(fg), lane)

    scale = HEAD_DIM ** -0.5 * LOG2E
    q = jnp.dot(hn, wqk_ref[:, 0:D_ATTN], preferred_element_type=F32)
    q_ref[...] = (q * scale).astype(BF16)
    k = jnp.dot(hn, wqk_ref[:, D_ATTN:2 * D_ATTN], preferred_element_type=F32)
    k_ref[...] = k.astype(BF16)
    cum3 = jnp.dot(tril_ref[...], parts, preferred_element_type=F32)
    vt = lax.dot_general(wvt_ref[...], hn, NT_DIMS, preferred_element_type=F32)
    vt_ref[...] = vt.astype(BF16)

    tot = cum3 + pltpu.roll(cum3, LANES - N_HEADS, 1) + pltpu.roll(cum3, LANES - 2 * N_HEADS, 1)
    tot3 = jnp.where(lane < N_HEADS, tot,
                     jnp.where(lane < 2 * N_HEADS, pltpu.roll(tot, N_HEADS, 1), pltpu.roll(tot, 2 * N_HEADS, 1)))

    @pl.when(pl.program_id(1) == 0)
    def _():
        carry_ref[...] = carry0_ref[...]

    cum = tot3 + carry_ref[...]
    last = cum[tm - 1:tm, :]
    carry_ref[...] = last
    carry_out_ref[...] = last
    ncp_ref[...] = _split3_by_lane_group(-LOG2E * cum, lane)

    a = jnp.dot(hn, wglu_ref[:, 0:D_CONV], preferred_element_type=F32) + bglu_ref[:, 0:D_CONV]
    gl = jnp.dot(hn, wglu_ref[:, D_CONV:2 * D_CONV], preferred_element_type=F32) + bglu_ref[:, D_CONV:2 * D_CONV]
    u_ref[...] = a * _sigmoid(gl)


def _in_proj(x, carry0, g, wqk, wvt, wfg, bfg, wglu, bglu, tm):
    b, s, d = x.shape
    tril = jnp.tril(jnp.ones((tm, tm), F32)).astype(BF16)
    row = lambda w: pl.BlockSpec((None, tm, w), lambda i, j: (i, j, 0))
    consts = (g, wqk, wvt, wfg, bfg, wglu, bglu, tril, carry0)
    return pl.pallas_call(
        _in_proj_kernel,
        grid=(b, s // tm),
        in_specs=[row(d)] + [_const_spec(c.shape) for c in consts],
        out_specs=[row(D_ATTN), row(D_ATTN),
                   pl.BlockSpec((None, D_ATTN, tm), lambda i, j: (i, 0, j)),
                   row(LANES), row(D_CONV),
                   pl.BlockSpec((None, 1, LANES), lambda i, j: (i, 0, 0))],
        out_shape=[jax.ShapeDtypeStruct((b, s, D_ATTN), BF16), jax.ShapeDtypeStruct((b, s, D_ATTN), BF16),
                   jax.ShapeDtypeStruct((b, D_ATTN, s), BF16), jax.ShapeDtypeStruct((b, s, LANES), BF16),
                   jax.ShapeDtypeStruct((b, s, D_CONV), F32), jax.ShapeDtypeStruct((b, 1, LANES), F32)],
        scratch_shapes=[pltpu.VMEM((1, LANES), F32)],
        compiler_params=pltpu.CompilerParams(dimension_semantics=("arbitrary", "arbitrary"),
                                             vmem_limit_bytes=VMEM_LIMIT),
        name="in_proj",
    )(x, *consts)


def _attn_kernel(q_ref, k_ref, ncp_ref, vt_ref, km_ref, ncpm_ref, vtm_ref, o_ref, *, tq):
    hb = pl.program_id(1)
    seq = q_ref.shape[0]
    lane = lax.broadcasted_iota(jnp.int32, (tq, LANES), 1)
    key_row = lax.broadcasted_iota(jnp.int32, (tq, tq), 0)
    qry_col = lax.broadcasted_iota(jnp.int32, (tq, tq), 1)
    causal = key_row <= qry_col
    ones_m = jnp.ones((BF16_ROWS, N_META), BF16)
    ones_k = jnp.ones((BF16_ROWS, tq), BF16)
    kaug_m = jnp.concatenate([km_ref[...], ncpm_ref[...]], axis=1)
    head_masks, sels, vtm_aug = [], [], []
    for h in range(HEADS_PER_BLOCK):
        head_masks.append((lane >= h * HEAD_DIM) & (lane < (h + 1) * HEAD_DIM))
        hg = hb * HEADS_PER_BLOCK + h
        pick = (lane == hg) | (lane == hg + N_HEADS) | (lane == hg + 2 * N_HEADS)
        sels.append(jnp.where(pick, 1.0, 0.0).astype(BF16))
        vtm_aug.append(jnp.concatenate([vtm_ref[h * HEAD_DIM:(h + 1) * HEAD_DIM, :], ones_m], axis=0))

    def scores(qi):
        nk = (qi + 1) * tq
        qb = q_ref[qi * tq:(qi + 1) * tq, :]
        qaug = jnp.concatenate(
            [jnp.concatenate([jnp.where(head_masks[h], qb, jnp.zeros_like(qb)), sels[h]], axis=1)
             for h in range(HEADS_PER_BLOCK)], axis=0)
        kaug = jnp.concatenate(
            [kaug_m, jnp.concatenate([k_ref[0:nk, :], ncp_ref[0:nk, :]], axis=1)], axis=0)
        return lax.dot_general(kaug, qaug, NT_DIMS, preferred_element_type=F32)

    def finish(qi, s_all):
        outs = []
        for h in range(HEADS_PER_BLOCK):
            s = s_all[:, h * tq:(h + 1) * tq]
            sm = s[0:N_META]
            m = jnp.max(sm, axis=0, keepdims=True)
            acc = jnp.dot(vtm_aug[h], jnp.exp2(sm - m).astype(BF16), preferred_element_type=F32)
            for j in range(qi + 1):
                c = s[N_META + j * tq:N_META + (j + 1) * tq]
                if j == qi:
                    c = jnp.where(causal, c, -jnp.inf)
                m_new = jnp.maximum(m, jnp.max(c, axis=0, keepdims=True))
                p = jnp.exp2(c - m_new).astype(BF16)
                vta = jnp.concatenate([vt_ref[h * HEAD_DIM:(h + 1) * HEAD_DIM, j * tq:(j + 1) * tq], ones_k],
                                      axis=0)
                acc = acc * jnp.exp2(m - m_new) + jnp.dot(vta, p, preferred_element_type=F32)
                m = m_new
            outs.append(acc[0:HEAD_DIM] * (1.0 / acc[HEAD_DIM:HEAD_DIM + 1]))
        o_ref[qi * tq:(qi + 1) * tq, :] = jnp.concatenate(outs, axis=0).T.astype(o_ref.dtype)

    nq = seq // tq
    order = list(range(nq - 1, -1, -1))
    s_next = scores(order[0])
    for i, qi in enumerate(order):
        s_cur = s_next
        if i + 1 < nq:
            s_next = scores(order[i + 1])
        finish(qi, s_cur)


def _attention(q, k, ncp, vt, k_meta, ncp_meta, vt_meta, tq):
    b, s, _ = q.shape
    seq = lambda: pl.BlockSpec((None, s, LANES), lambda i, hb: (i, 0, hb))
    return pl.pallas_call(
        functools.partial(_attn_kernel, tq=tq),
        grid=(b, N_HEAD_BLOCKS),
        in_specs=[
            seq(), seq(),
            pl.BlockSpec((None, s, LANES), lambda i, hb: (i, 0, 0)),
            pl.BlockSpec((None, LANES, s), lambda i, hb: (i, hb, 0)),
            pl.BlockSpec((N_META, LANES), lambda i, hb: (0, hb)),
            pl.BlockSpec((N_META, LANES), lambda i, hb: (0, 0)),
            pl.BlockSpec((LANES, N_META), lambda i, hb: (hb, 0)),
        ],
        out_specs=seq(),
        out_shape=jax.ShapeDtypeStruct((b, s, D_ATTN), BF16),
        compiler_params=pltpu.CompilerParams(dimension_semantics=("arbitrary", "arbitrary"),
                                             vmem_limit_bytes=VMEM_LIMIT),
        name="fox_attention",
    )(q, k, ncp, vt, k_meta, ncp_meta, vt_meta)


def _zero_like_tile(v):
    bits = pltpu.bitcast(v, jnp.uint32)
    bits = lax.shift_right_logical(lax.shift_right_logical(bits, jnp.uint32(16)), jnp.uint32(16))
    return pltpu.bitcast(bits, F32)


def _conv_unit(win_ref, w_ref, b_ref, y_ref, r0, c, after):
    off = CONV_HALO - (CONV_WIDTH - 1)
    cs = slice(c * LANES, (c + 1) * LANES)
    win = win_ref[r0:r0 + CONV_ROWS + CONV_HALO, cs]
    bias = b_ref[:, cs]
    if after is not None:
        bias = bias + _zero_like_tile(after)[0:1, :]
    acc = jnp.zeros((CONV_ROWS, LANES), F32) + bias
    nwin = win.shape[0]
    for shift in range(SUBLANES):
        taps = [j for j in range(CONV_WIDTH) if (off + j) % SUBLANES == shift]
        shifted = pltpu.roll(win, nwin - shift, 0) if shift else win
        for j in taps:
            base = (off + j) // SUBLANES * SUBLANES
            acc = acc + shifted[base:base + CONV_ROWS] * w_ref[j:j + 1, cs]
    y_ref[r0:r0 + CONV_ROWS, cs] = acc
    return acc[CONV_ROWS - SUBLANES:CONV_ROWS, :]


def _back_kernel(x_ref, at_ref, u_ref, halo_ref, mhalo_ref, cw_ref, cb_ref, lg_ref, lb_ref, g_ref, wg_ref,
                 wao_ref, wco_ref, bco_ref, wout_ref, g2_ref, wup_ref, wdn_ref, gf_ref, o_ref,
                 win_ref, y_ref, cact_ref, *, tiles_per_row, n_tiles):
    tm = x_ref.shape[0]
    t = pl.program_id(0)

    @pl.when(t == 0)
    def _():
        cact_ref[...] = jnp.zeros_like(cact_ref)

    row_start = lax.rem(jnp.minimum(t, n_tiles - 1), tiles_per_row) == 0
    win_ref[0:CONV_HALO, :] = jnp.where(row_start, mhalo_ref[...], halo_ref[...])
    win_ref[CONV_HALO:CONV_HALO + tm, :] = u_ref[...]
    units = [(r0, c) for r0 in range(0, tm, CONV_ROWS) for c in range(D_CONV // LANES)]

    def conv_group(n_units, after):
        tok = after
        for _ in range(n_units):
            r0, c = units.pop(0)
            tok = _conv_unit(win_ref, cw_ref, cb_ref, y_ref, r0, c, tok)
        return tok

    last_tile = lambda v: v[v.shape[0] - SUBLANES:, v.shape[1] - LANES:]
    zeros_bf16 = lambda tok, shape: jnp.tile(
        jnp.concatenate([_zero_like_tile(tok)] * 2, axis=0).astype(BF16),
        (shape[0] // BF16_ROWS, shape[1] // LANES))
    slot = lax.rem(t, 2)
    cact_prev = cact_ref[1 - slot]

    x = x_ref[...]
    hn = _rms_norm(x, g_ref[...]).astype(BF16)
    gates = jnp.dot(hn, wg_ref[...], preferred_element_type=F32)
    tok0 = conv_group(3, None)
    at = at_ref[...]
    a = jnp.dot(at + zeros_bf16(tok0, at.shape), wao_ref[...], preferred_element_type=F32)
    tok1 = conv_group(2, last_tile(gates))
    c = jnp.dot(cact_prev + zeros_bf16(tok1, cact_prev.shape), wco_ref[...], preferred_element_type=F32)
    tok2 = conv_group(2, last_tile(a))
    bco = bco_ref[...] + jnp.tile(_zero_like_tile(tok2)[0:1, :], (1, D_MODEL // LANES))
    ga = _sigmoid(gates[:, 0:D_MODEL])
    gc = _sigmoid(gates[:, D_MODEL:2 * D_MODEL])
    mix = (ga * a + gc * (c + bco)).astype(BF16)
    h1 = x + jnp.dot(mix, wout_ref[...], preferred_element_type=F32)
    tok3 = conv_group(1, last_tile(c))
    ms = jnp.mean(h1 * h1, axis=-1, keepdims=True)
    inv = lax.rsqrt(ms + RMS_EPS) + jnp.tile(_zero_like_tile(tok3), (tm // SUBLANES, 1))[:, 0:1]
    hn2 = (h1 * inv * g2_ref[...]).astype(BF16)
    up = jnp.dot(hn2, wup_ref[...], preferred_element_type=F32)
    tok4 = conv_group(5, last_tile(h1))
    floor = jnp.tile(_zero_like_tile(tok4), (tm // SUBLANES, D_FF // LANES))
    act = jnp.square(jnp.maximum(up, floor)).astype(BF16)
    h2 = h1 + jnp.dot(act, wdn_ref[...], preferred_element_type=F32)
    conv_group(3, last_tile(up))
    y = y_ref[...]
    mu = jnp.mean(y, axis=-1, keepdims=True)
    yc = y - mu
    var = jnp.mean(yc * yc, axis=-1, keepdims=True)
    z = yc * lax.rsqrt(var + LN_EPS) * lg_ref[...] + lb_ref[...]
    cact_ref[slot] = (z * _sigmoid(z)).astype(BF16)
    o_ref[...] = _rms_norm(h2, gf_ref[...])


def _back(x, attn, u, meta_halo, consts, tm):
    b, s, d = x.shape
    nj = s // tm
    n_tiles = b * nj
    halo_blocks = tm // CONV_HALO
    prev = lambda t: jnp.maximum(t - 1, 0)
    cur = lambda t: jnp.minimum(t, n_tiles - 1)
    chain = lambda: pl.BlockSpec((None, tm, d), lambda t: (prev(t) // nj, prev(t) % nj, 0))
    return pl.pallas_call(
        functools.partial(_back_kernel, tiles_per_row=nj, n_tiles=n_tiles),
        grid=(n_tiles + 1,),
        in_specs=[chain(), chain(),
                  pl.BlockSpec((None, tm, d), lambda t: (cur(t) // nj, cur(t) % nj, 0)),
                  pl.BlockSpec((None, CONV_HALO, d),
                               lambda t: (cur(t) // nj, jnp.maximum(cur(t) % nj * halo_blocks - 1, 0), 0)),
                  _const_spec(meta_halo.shape)] + [_const_spec(c.shape) for c in consts],
        out_specs=chain(),
        out_shape=jax.ShapeDtypeStruct((b, s, d), F32),
        scratch_shapes=[pltpu.VMEM((CONV_HALO + tm, d), F32), pltpu.VMEM((tm, d), F32),
                        pltpu.VMEM((2, tm, d), BF16)],
        compiler_params=pltpu.CompilerParams(dimension_semantics=("arbitrary",),
                                             vmem_limit_bytes=VMEM_LIMIT),
        name="conv_merge_mlp",
    )(x, attn, u, u, meta_halo, *consts)


def kernel(x, meta_tokens, norm_mix_gain, w_in, b_forget, w_attn_out, b_glu, conv_dw_w, conv_dw_b,
           conv_ln_gain, conv_ln_bias, w_conv_out, b_conv_out, w_out, norm_mlp_gain, w_mlp_up,
           w_mlp_down, final_norm_gain):
    assert w_in.shape[0] == 1, "single layer"
    o_v, o_fg, o_glu = 2 * D_ATTN, 3 * D_ATTN, 3 * D_ATTN + N_HEADS
    o_gate = o_glu + 2 * D_CONV
    w = w_in[0]
    wqk = w[:, 0:o_v].astype(BF16)
    wvt = w[:, o_v:o_fg].T.astype(BF16)
    lane_pad = LANES - N_SPLIT * N_HEADS
    wfg = jnp.pad(jnp.tile(w[:, o_fg:o_glu], (1, N_SPLIT)), ((0, 0), (0, lane_pad))).astype(BF16)
    bfg = jnp.pad(jnp.tile(b_forget[0], N_SPLIT), (0, lane_pad))[None, :]
    wglu = w[:, o_glu:o_gate].astype(BF16)
    wgate = w[:, o_gate:].astype(BF16)
    g_mix = norm_mix_gain[0][None, :]
    bglu = b_glu[0][None, :]

    proj = functools.partial(_in_proj, g=g_mix, wqk=wqk, wvt=wvt, wfg=wfg, bfg=bfg, wglu=wglu, bglu=bglu)
    _, k_m, vt_m, ncp_m, u_m, carry_m = proj(meta_tokens[None].astype(x.dtype), jnp.zeros((1, LANES), F32),
                                              tm=N_META)
    q, k, vt, ncp, u, _ = proj(x, carry_m[0], tm=512)

    attn = _attention(q, k, ncp, vt, k_m[0], ncp_m[0], vt_m[0], tq=256)

    meta_halo = jnp.concatenate([jnp.zeros((CONV_HALO - N_META, D_CONV), F32), u_m[0]], axis=0)
    vec = lambda v: v[None, :]
    consts = (conv_dw_w[0], vec(conv_dw_b[0]), vec(conv_ln_gain[0]), vec(conv_ln_bias[0]), g_mix, wgate,
              w_attn_out[0].astype(BF16), w_conv_out[0].astype(BF16), vec(b_conv_out[0]), w_out[0].astype(BF16),
              vec(norm_mlp_gain[0]), w_mlp_up[0].astype(BF16), w_mlp_down[0].astype(BF16), vec(final_norm_gain))
    return _back(x, attn, u, meta_halo, consts, tm=256)
```

```python
import functools

import jax
import jax.numpy as jnp
from jax import lax
from jax.experimental import pallas as pl
from jax.experimental.pallas import tpu as pltpu

D_MODEL = 1024
N_META = 16
N_HEADS = 16
HEAD_DIM = 64
D_ATTN = N_HEADS * HEAD_DIM
D_CONV = D_MODEL
CONV_WIDTH = 31
D_FF = 4 * D_MODEL
RMS_EPS = 1e-6
LN_EPS = 1e-5
LOG2E = 1.4426950408889634

LANES = 128
SUBLANES = 8
BF16_ROWS = 16
HEADS_PER_BLOCK = LANES // HEAD_DIM
N_HEAD_BLOCKS = N_HEADS // HEADS_PER_BLOCK
SCORE_LOOKAHEAD = 2
N_SPLIT = 3
CONV_HALO = 32
CONV_ROWS = 128
CONV_GROUPS = (10, 2, 2, 1, 1, 1, 1, 1, 1, 1, 1, 1, 1)
VMEM_LIMIT = 56 * 1024 * 1024

BF16 = jnp.bfloat16
F32 = jnp.float32
NT_DIMS = (((1,), (1,)), ((), ()))


def _const_spec(shape):
    nd = len(shape)
    return pl.BlockSpec(shape, lambda *_: (0,) * nd, pipeline_mode=pl.Buffered(1))


def _sigmoid(x):
    return 1.0 / (1.0 + jnp.exp(-x))


def _rms_norm(x, g):
    ms = jnp.mean(x * x, axis=-1, keepdims=True)
    return x * lax.rsqrt(ms + RMS_EPS) * g


def _log_sigmoid(x):
    return jnp.minimum(x, 0.0) - jnp.log(1.0 + jnp.exp(-jnp.abs(x)))


def _split3_by_lane_group(x, lane):
    hi = x.astype(BF16).astype(F32)
    r1 = x - hi
    mid = r1.astype(BF16).astype(F32)
    lo = (r1 - mid).astype(BF16).astype(F32)
    out = jnp.where(lane < N_HEADS, hi,
                    jnp.where(lane < 2 * N_HEADS, mid, jnp.where(lane < 3 * N_HEADS, lo, 0.0)))
    return out.astype(BF16)


def _in_proj_kernel(x_ref, g_ref, wqk_ref, wvt_ref, wfg_ref, bfg_ref, wglu_ref, bglu_ref, tril_ref, carry0_ref,
                    q_ref, k_ref, vt_ref, ncp_ref, u_ref, carry_out_ref, carry_ref):
    tm = x_ref.shape[0]
    hn = _rms_norm(x_ref[...], g_ref[...]).astype(BF16)
    lane = lax.broadcasted_iota(jnp.int32, (tm, LANES), 1)
    fg = jnp.dot(hn, wfg_ref[...], preferred_element_type=F32) + bfg_ref[...]
    parts = _split3_by_lane_group(_log_sigmoid(fg), lane)

    scale = HEAD_DIM ** -0.5 * LOG2E
    q = jnp.dot(hn, wqk_ref[:, 0:D_ATTN], preferred_element_type=F32)
    q_ref[...] = (q * scale).astype(BF16)
    k = jnp.dot(hn, wqk_ref[:, D_ATTN:2 * D_ATTN], preferred_element_type=F32)
    k_ref[...] = k.astype(BF16)
    cum3 = jnp.dot(tril_ref[...], parts, preferred_element_type=F32)
    vt = lax.dot_general(wvt_ref[...], hn, NT_DIMS, preferred_element_type=F32)
    vt_ref[...] = vt.astype(BF16)

    tot = cum3 + pltpu.roll(cum3, LANES - N_HEADS, 1) + pltpu.roll(cum3, LANES - 2 * N_HEADS, 1)
    tot3 = jnp.where(lane < N_HEADS, tot,
                     jnp.where(lane < 2 * N_HEADS, pltpu.roll(tot, N_HEADS, 1), pltpu.roll(tot, 2 * N_HEADS, 1)))

    @pl.when(pl.program_id(1) == 0)
    def _():
        carry_ref[...] = carry0_ref[...]

    cum = tot3 + carry_ref[...]
    last = cum[tm - 1:tm, :]
    carry_ref[...] = last
    carry_out_ref[...] = last
    ncp_ref[...] = _split3_by_lane_group(-LOG2E * cum, lane)

    a = jnp.dot(hn, wglu_ref[:, 0:D_CONV], preferred_element_type=F32) + bglu_ref[:, 0:D_CONV]
    gl = jnp.dot(hn, wglu_ref[:, D_CONV:2 * D_CONV], preferred_element_type=F32) + bglu_ref[:, D_CONV:2 * D_CONV]
    u_ref[...] = a * _sigmoid(gl)


def _in_proj(x, carry0, g, wqk, wvt, wfg, bfg, wglu, bglu, tm):
    b, s, d = x.shape
    tril = jnp.tril(jnp.ones((tm, tm), F32)).astype(BF16)
    row = lambda w: pl.BlockSpec((None, tm, w), lambda i, j: (i, j, 0))
    consts = (g, wqk, wvt, wfg, bfg, wglu, bglu, tril, carry0)
    return pl.pallas_call(
        _in_proj_kernel,
        grid=(b, s // tm),
        in_specs=[row(d)] + [_const_spec(c.shape) for c in consts],
        out_specs=[row(D_ATTN), row(D_ATTN),
                   pl.BlockSpec((None, D_ATTN, tm), lambda i, j: (i, 0, j)),
                   row(LANES), row(D_CONV),
                   pl.BlockSpec((None, 1, LANES), lambda i, j: (i, 0, 0))],
        out_shape=[jax.ShapeDtypeStruct((b, s, D_ATTN), BF16), jax.ShapeDtypeStruct((b, s, D_ATTN), BF16),
                   jax.ShapeDtypeStruct((b, D_ATTN, s), BF16), jax.ShapeDtypeStruct((b, s, LANES), BF16),
                   jax.ShapeDtypeStruct((b, s, D_CONV), F32), jax.ShapeDtypeStruct((b, 1, LANES), F32)],
        scratch_shapes=[pltpu.VMEM((1, LANES), F32)],
        compiler_params=pltpu.CompilerParams(dimension_semantics=("arbitrary", "arbitrary"),
                                             vmem_limit_bytes=VMEM_LIMIT),
        name="in_proj",
    )(x, *consts)


def _attn_kernel(q_ref, k_ref, ncp_ref, vt_ref, km_ref, ncpm_ref, vtm_ref, o_ref, *, tq):
    hb = pl.program_id(1)
    seq = q_ref.shape[0]
    lane = lax.broadcasted_iota(jnp.int32, (tq, LANES), 1)
    key_row = lax.broadcasted_iota(jnp.int32, (tq, tq), 0)
    qry_col = lax.broadcasted_iota(jnp.int32, (tq, tq), 1)
    causal = key_row <= qry_col
    ones_m = jnp.ones((BF16_ROWS, N_META), BF16)
    ones_k = jnp.ones((BF16_ROWS, tq), BF16)
    kaug_m = jnp.concatenate([km_ref[...], ncpm_ref[...]], axis=1)
    head_masks, sels, vtm_aug = [], [], []
    for h in range(HEADS_PER_BLOCK):
        head_masks.append((lane >= h * HEAD_DIM) & (lane < (h + 1) * HEAD_DIM))
        hg = hb * HEADS_PER_BLOCK + h
        pick = (lane == hg) | (lane == hg + N_HEADS) | (lane == hg + 2 * N_HEADS)
        sels.append(jnp.where(pick, 1.0, 0.0).astype(BF16))
        vtm_aug.append(jnp.concatenate([vtm_ref[h * HEAD_DIM:(h + 1) * HEAD_DIM, :], ones_m], axis=0))

    def scores(qi):
        nk = (qi + 1) * tq
        qb = q_ref[qi * tq:(qi + 1) * tq, :]
        qaug = jnp.concatenate(
            [jnp.concatenate([jnp.where(head_masks[h], qb, jnp.zeros_like(qb)), sels[h]], axis=1)
             for h in range(HEADS_PER_BLOCK)], axis=0)
        kaug = jnp.concatenate(
            [kaug_m, jnp.concatenate([k_ref[0:nk, :], ncp_ref[0:nk, :]], axis=1)], axis=0)
        return lax.dot_general(kaug, qaug, NT_DIMS, preferred_element_type=F32)

    def finish(qi, s_all):
        outs = []
        for h in range(HEADS_PER_BLOCK):
            s = s_all[:, h * tq:(h + 1) * tq]
            sm = s[0:N_META]
            m = jnp.max(sm, axis=0, keepdims=True)
            acc = jnp.dot(vtm_aug[h], jnp.exp2(sm - m).astype(BF16), preferred_element_type=F32)
            for j in range(qi + 1):
                c = s[N_META + j * tq:N_META + (j + 1) * tq]
                if j == qi:
                    c = jnp.where(causal, c, -jnp.inf)
                m_new = jnp.maximum(m, jnp.max(c, axis=0, keepdims=True))
                p = jnp.exp2(c - m_new).astype(BF16)
                vta = jnp.concatenate([vt_ref[h * HEAD_DIM:(h + 1) * HEAD_DIM, j * tq:(j + 1) * tq], ones_k],
                                      axis=0)
                acc = acc * jnp.exp2(m - m_new) + jnp.dot(vta, p, preferred_element_type=F32)
                m = m_new
            outs.append(acc[0:HEAD_DIM] * (1.0 / acc[HEAD_DIM:HEAD_DIM + 1]))
        o_ref[qi * tq:(qi + 1) * tq, :] = jnp.concatenate(outs, axis=0).T.astype(o_ref.dtype)

    nq = seq // tq
    order = list(range(nq - 1, -1, -1))
    pending = [scores(qi) for qi in order[:SCORE_LOOKAHEAD]]
    for i, qi in enumerate(order):
        s_cur = pending.pop(0)
        if i + SCORE_LOOKAHEAD < nq:
            pending.append(scores(order[i + SCORE_LOOKAHEAD]))
        finish(qi, s_cur)


def _attention(q, k, ncp, vt, k_meta, ncp_meta, vt_meta, tq):
    b, s, _ = q.shape
    seq = lambda: pl.BlockSpec((None, s, LANES), lambda i, hb: (i, 0, hb))
    return pl.pallas_call(
        functools.partial(_attn_kernel, tq=tq),
        grid=(b, N_HEAD_BLOCKS),
        in_specs=[
            seq(), seq(),
            pl.BlockSpec((None, s, LANES), lambda i, hb: (i, 0, 0)),
            pl.BlockSpec((None, LANES, s), lambda i, hb: (i, hb, 0)),
            pl.BlockSpec((N_META, LANES), lambda i, hb: (0, hb)),
            pl.BlockSpec((N_META, LANES), lambda i, hb: (0, 0)),
            pl.BlockSpec((LANES, N_META), lambda i, hb: (hb, 0)),
        ],
        out_specs=seq(),
        out_shape=jax.ShapeDtypeStruct((b, s, D_ATTN), BF16),
        compiler_params=pltpu.CompilerParams(dimension_semantics=("arbitrary", "arbitrary"),
                                             vmem_limit_bytes=VMEM_LIMIT),
        name="fox_attention",
    )(q, k, ncp, vt, k_meta, ncp_meta, vt_meta)


def _zero_like_tile(v):
    bits = pltpu.bitcast(v, jnp.uint32)
    bits = lax.shift_right_logical(lax.shift_right_logical(bits, jnp.uint32(16)), jnp.uint32(16))
    return pltpu.bitcast(bits, F32)


def _shift_pass(win_ref, sh_ref, c, after):
    cs = slice(c * LANES, (c + 1) * LANES)
    win = win_ref[:, cs]
    if after is not None:
        win = win + _zero_like_tile(after)[0:1, :]
    n = win.shape[0]
    tok = None
    for shift in range(1, SUBLANES):
        rolled = pltpu.roll(win, n - shift, 0)
        sh_ref[shift - 1, :, cs] = rolled
        tok = rolled[0:SUBLANES]
    return tok


def _conv_unit(win_ref, sh_ref, w_ref, b_ref, y_ref, r0, c, after):
    off = CONV_HALO - (CONV_WIDTH - 1)
    cs = slice(c * LANES, (c + 1) * LANES)
    bias = b_ref[:, cs]
    if after is not None:
        bias = bias + _zero_like_tile(after)[0:1, :]
    acc = jnp.zeros((CONV_ROWS, LANES), F32) + bias
    for j in range(CONV_WIDTH):
        shift, base = (off + j) % SUBLANES, (off + j) // SUBLANES * SUBLANES
        rows = slice(r0 + base, r0 + base + CONV_ROWS)
        src = win_ref[rows, cs] if shift == 0 else sh_ref[shift - 1, rows, cs]
        acc = acc + src * w_ref[j:j + 1, cs]
    y_ref[r0:r0 + CONV_ROWS, cs] = acc
    return acc[CONV_ROWS - SUBLANES:CONV_ROWS, :]


def _back_kernel(x_ref, at_ref, u_ref, halo_ref, mhalo_ref, cw_ref, cb_ref, lg_ref, lb_ref, g_ref, wg_ref,
                 wao_ref, wco_ref, bco_ref, wout_ref, g2_ref, wup_ref, wdn_ref, gf_ref, o_ref,
                 win_ref, y_ref, cact_ref, sh_ref, *, tiles_per_row, n_tiles):
    tm = x_ref.shape[0]
    t = pl.program_id(0)

    @pl.when(t == 0)
    def _():
        cact_ref[...] = jnp.zeros_like(cact_ref)

    row_start = lax.rem(jnp.minimum(t, n_tiles - 1), tiles_per_row) == 0
    win_ref[0:CONV_HALO, :] = jnp.where(row_start, mhalo_ref[...], halo_ref[...])
    win_ref[CONV_HALO:CONV_HALO + tm, :] = u_ref[...]
    units = [("shift", c) for c in range(D_CONV // LANES)]
    units += [(r0, c) for c in range(D_CONV // LANES) for r0 in range(0, tm, CONV_ROWS)]

    def conv_group(n_units, after):
        tok = after
        for _ in range(n_units):
            r0, c = units.pop(0)
            if r0 == "shift":
                tok = _shift_pass(win_ref, sh_ref, c, tok)
            else:
                tok = _conv_unit(win_ref, sh_ref, cw_ref, cb_ref, y_ref, r0, c, tok)
        return tok

    last_tile = lambda v: v[v.shape[0] - SUBLANES:, v.shape[1] - LANES:]
    zeros_bf16 = lambda tok, shape: jnp.tile(
        jnp.concatenate([_zero_like_tile(tok)] * 2, axis=0).astype(BF16),
        (shape[0] // BF16_ROWS, shape[1] // LANES))
    slot = lax.rem(t, 2)
    cact_prev = cact_ref[1 - slot]

    def after(tok, v):
        return v if tok is None else v + zeros_bf16(tok, v.shape)

    x = x_ref[...]
    hn = _rms_norm(x, g_ref[...]).astype(BF16)
    ga_raw = jnp.dot(hn, wg_ref[:, 0:D_MODEL], preferred_element_type=F32)
    tok = conv_group(CONV_GROUPS[0], None)
    gc_raw = jnp.dot(after(tok, hn), wg_ref[:, D_MODEL:2 * D_MODEL], preferred_element_type=F32)
    tok = conv_group(CONV_GROUPS[1], last_tile(ga_raw))
    a = jnp.dot(after(tok, at_ref[...]), wao_ref[...], preferred_element_type=F32)
    tok = conv_group(CONV_GROUPS[2], last_tile(gc_raw))
    c = jnp.dot(after(tok, cact_prev), wco_ref[...], preferred_element_type=F32) + bco_ref[...]
    tok = conv_group(CONV_GROUPS[3], last_tile(a))
    mix = (_sigmoid(ga_raw) * a + _sigmoid(gc_raw) * c).astype(BF16)
    h1 = x + jnp.dot(after(tok, mix), wout_ref[...], preferred_element_type=F32)
    tok = conv_group(CONV_GROUPS[4], last_tile(c))
    hn2 = _rms_norm(h1, g2_ref[...]).astype(BF16)
    prev_result = last_tile(h1)
    h2 = h1
    n_pieces = 4
    piece = D_FF // n_pieces
    acts = []
    for i in range(n_pieces):
        up = jnp.dot(after(tok, hn2), wup_ref[:, i * piece:(i + 1) * piece], preferred_element_type=F32)
        tok = conv_group(CONV_GROUPS[5 + i], prev_result)
        prev_result = last_tile(up)
        acts.append(jnp.square(jnp.maximum(up, 0.0)).astype(BF16))
    for i in range(n_pieces):
        dn = jnp.dot(after(tok, acts[i]), wdn_ref[i * piece:(i + 1) * piece, :], preferred_element_type=F32)
        tok = conv_group(CONV_GROUPS[9 + i], prev_result)
        prev_result = last_tile(dn)
        h2 = h2 + dn
    assert not units
    y = y_ref[...]
    mu = jnp.mean(y, axis=-1, keepdims=True)
    yc = y - mu
    var = jnp.mean(yc * yc, axis=-1, keepdims=True)
    z = yc * lax.rsqrt(var + LN_EPS) * lg_ref[...] + lb_ref[...]
    cact_ref[slot] = (z * _sigmoid(z)).astype(BF16)
    o_ref[...] = _rms_norm(h2, gf_ref[...])


def _back(x, attn, u, meta_halo, consts, tm):
    b, s, d = x.shape
    nj = s // tm
    n_tiles = b * nj
    halo_blocks = tm // CONV_HALO
    prev = lambda t: jnp.maximum(t - 1, 0)
    cur = lambda t: jnp.minimum(t, n_tiles - 1)
    chain = lambda: pl.BlockSpec((None, tm, d), lambda t: (prev(t) // nj, prev(t) % nj, 0))
    return pl.pallas_call(
        functools.partial(_back_kernel, tiles_per_row=nj, n_tiles=n_tiles),
        grid=(n_tiles + 1,),
        in_specs=[chain(), chain(),
                  pl.BlockSpec((None, tm, d), lambda t: (cur(t) // nj, cur(t) % nj, 0)),
                  pl.BlockSpec((None, CONV_HALO, d),
                               lambda t: (cur(t) // nj, jnp.maximum(cur(t) % nj * halo_blocks - 1, 0), 0)),
                  _const_spec(meta_halo.shape)] + [_const_spec(c.shape) for c in consts],
        out_specs=chain(),
        out_shape=jax.ShapeDtypeStruct((b, s, d), F32),
        scratch_shapes=[pltpu.VMEM((CONV_HALO + tm, d), F32), pltpu.VMEM((tm, d), F32),
                        pltpu.VMEM((2, tm, d), BF16), pltpu.VMEM((SUBLANES - 1, CONV_HALO + tm, d), F32)],
        compiler_params=pltpu.CompilerParams(dimension_semantics=("arbitrary",),
                                             vmem_limit_bytes=VMEM_LIMIT),
        name="conv_merge_mlp",
    )(x, attn, u, u, meta_halo, *consts)


def kernel(x, meta_tokens, norm_mix_gain, w_in, b_forget, w_attn_out, b_glu, conv_dw_w, conv_dw_b,
           conv_ln_gain, conv_ln_bias, w_conv_out, b_conv_out, w_out, norm_mlp_gain, w_mlp_up,
           w_mlp_down, final_norm_gain):
    assert w_in.shape[0] == 1, "single layer"
    o_v, o_fg, o_glu = 2 * D_ATTN, 3 * D_ATTN, 3 * D_ATTN + N_HEADS
    o_gate = o_glu + 2 * D_CONV
    w = w_in[0].astype(BF16)
    wqk = w[:, 0:o_v]
    wvt = w[:, o_v:o_fg].T
    lane_pad = LANES - N_SPLIT * N_HEADS
    wfg = jnp.pad(jnp.tile(w[:, o_fg:o_glu], (1, N_SPLIT)), ((0, 0), (0, lane_pad)))
    bfg = jnp.pad(jnp.tile(b_forget[0], N_SPLIT), (0, lane_pad))[None, :]
    wglu = w[:, o_glu:o_gate]
    wgate = w[:, o_gate:]
    g_mix = norm_mix_gain[0][None, :]
    bglu = b_glu[0][None, :]

    proj = functools.partial(_in_proj, g=g_mix, wqk=wqk, wvt=wvt, wfg=wfg, bfg=bfg, wglu=wglu, bglu=bglu)
    _, k_m, vt_m, ncp_m, u_m, carry_m = proj(meta_tokens[None].astype(x.dtype), jnp.zeros((1, LANES), F32),
                                              tm=N_META)
    q, k, vt, ncp, u, _ = proj(x, carry_m[0], tm=512)

    attn = _attention(q, k, ncp, vt, k_m[0], ncp_m[0], vt_m[0], tq=256)

    meta_halo = jnp.concatenate([jnp.zeros((CONV_HALO - N_META, D_CONV), F32), u_m[0]], axis=0)
    vec = lambda v: v[None, :]
    consts = (conv_dw_w[0], vec(conv_dw_b[0]), vec(conv_ln_gain[0]), vec(conv_ln_bias[0]), g_mix, wgate,
              w_attn_out[0].astype(BF16), w_conv_out[0].astype(BF16), vec(b_conv_out[0]), w_out[0].astype(BF16),
              vec(norm_mlp_gain[0]), w_mlp_up[0].astype(BF16), w_mlp_down[0].astype(BF16), vec(final_norm_gain))
    return _back(x, attn, u, meta_halo, consts, tm=256)
```

```python
import functools

import jax
import jax.numpy as jnp
from jax import lax
from jax.experimental import pallas as pl
from jax.experimental.pallas import tpu as pltpu

D_MODEL = 1024
N_META = 16
N_HEADS = 16
HEAD_DIM = 64
D_ATTN = N_HEADS * HEAD_DIM
D_CONV = D_MODEL
CONV_WIDTH = 31
D_FF = 4 * D_MODEL
RMS_EPS = 1e-6
LN_EPS = 1e-5
LOG2E = 1.4426950408889634

LANES = 128
SUBLANES = 8
BF16_ROWS = 16
HEADS_PER_BLOCK = LANES // HEAD_DIM
N_HEAD_BLOCKS = N_HEADS // HEADS_PER_BLOCK
SCORE_LOOKAHEAD = 2
N_SPLIT = 3
CONV_HALO = 32
CONV_ROWS = 64
CONV_GROUPS = (12, 4, 4, 2, 2, 2, 2, 2, 2, 2, 2, 2, 2)
VMEM_LIMIT = 56 * 1024 * 1024

BF16 = jnp.bfloat16
F32 = jnp.float32
NT_DIMS = (((1,), (1,)), ((), ()))


def _const_spec(shape):
    nd = len(shape)
    return pl.BlockSpec(shape, lambda *_: (0,) * nd, pipeline_mode=pl.Buffered(1))


def _sigmoid(x):
    return 1.0 / (1.0 + jnp.exp(-x))


def _rms_norm(x, g):
    ms = jnp.mean(x * x, axis=-1, keepdims=True)
    return x * lax.rsqrt(ms + RMS_EPS) * g


def _log_sigmoid(x):
    return jnp.minimum(x, 0.0) - jnp.log(1.0 + jnp.exp(-jnp.abs(x)))


def _split3_by_lane_group(x, lane):
    hi = x.astype(BF16).astype(F32)
    r1 = x - hi
    mid = r1.astype(BF16).astype(F32)
    lo = (r1 - mid).astype(BF16).astype(F32)
    out = jnp.where(lane < N_HEADS, hi,
                    jnp.where(lane < 2 * N_HEADS, mid, jnp.where(lane < 3 * N_HEADS, lo, 0.0)))
    return out.astype(BF16)


def _in_proj_kernel(x_ref, g_ref, wqk_ref, wvt_ref, wfg_ref, bfg_ref, wglu_ref, bglu_ref, tril_ref, carry0_ref,
                    q_ref, k_ref, vt_ref, ncp_ref, u_ref, carry_out_ref, carry_ref):
    tm = x_ref.shape[0]
    hn = _rms_norm(x_ref[...], g_ref[...]).astype(BF16)
    lane = lax.broadcasted_iota(jnp.int32, (tm, LANES), 1)
    fg = jnp.dot(hn, wfg_ref[...], preferred_element_type=F32) + bfg_ref[...]
    parts = _split3_by_lane_group(_log_sigmoid(fg), lane)

    a = jnp.dot(hn, wglu_ref[:, 0:D_CONV], preferred_element_type=F32) + bglu_ref[:, 0:D_CONV]
    gl = jnp.dot(hn, wglu_ref[:, D_CONV:2 * D_CONV], preferred_element_type=F32) + bglu_ref[:, D_CONV:2 * D_CONV]
    u_ref[...] = a * _sigmoid(gl)

    scale = HEAD_DIM ** -0.5 * LOG2E
    q = jnp.dot(hn, wqk_ref[:, 0:D_ATTN], preferred_element_type=F32)
    q_ref[...] = (q * scale).astype(BF16)
    k = jnp.dot(hn, wqk_ref[:, D_ATTN:2 * D_ATTN], preferred_element_type=F32)
    k_ref[...] = k.astype(BF16)
    cum3 = jnp.dot(tril_ref[...], parts, preferred_element_type=F32)
    vt = lax.dot_general(wvt_ref[...], hn, NT_DIMS, preferred_element_type=F32)
    vt_ref[...] = vt.astype(BF16)

    tot = cum3 + pltpu.roll(cum3, LANES - N_HEADS, 1) + pltpu.roll(cum3, LANES - 2 * N_HEADS, 1)
    tot3 = jnp.where(lane < N_HEADS, tot,
                     jnp.where(lane < 2 * N_HEADS, pltpu.roll(tot, N_HEADS, 1), pltpu.roll(tot, 2 * N_HEADS, 1)))

    @pl.when(pl.program_id(1) == 0)
    def _():
        carry_ref[...] = carry0_ref[...]

    cum = tot3 + carry_ref[...]
    last = cum[tm - 1:tm, :]
    carry_ref[...] = last
    carry_out_ref[...] = last
    ncp_ref[...] = _split3_by_lane_group(-LOG2E * cum, lane)


def _in_proj(x, carry0, g, wqk, wvt, wfg, bfg, wglu, bglu, tm):
    b, s, d = x.shape
    tril = jnp.tril(jnp.ones((tm, tm), F32)).astype(BF16)
    row = lambda w: pl.BlockSpec((None, tm, w), lambda i, j: (i, j, 0))
    consts = (g, wqk, wvt, wfg, bfg, wglu, bglu, tril, carry0)
    return pl.pallas_call(
        _in_proj_kernel,
        grid=(b, s // tm),
        in_specs=[row(d)] + [_const_spec(c.shape) for c in consts],
        out_specs=[row(D_ATTN), row(D_ATTN),
                   pl.BlockSpec((None, D_ATTN, tm), lambda i, j: (i, 0, j)),
                   row(LANES), row(D_CONV),
                   pl.BlockSpec((None, 1, LANES), lambda i, j: (i, 0, 0))],
        out_shape=[jax.ShapeDtypeStruct((b, s, D_ATTN), BF16), jax.ShapeDtypeStruct((b, s, D_ATTN), BF16),
                   jax.ShapeDtypeStruct((b, D_ATTN, s), BF16), jax.ShapeDtypeStruct((b, s, LANES), BF16),
                   jax.ShapeDtypeStruct((b, s, D_CONV), F32), jax.ShapeDtypeStruct((b, 1, LANES), F32)],
        scratch_shapes=[pltpu.VMEM((1, LANES), F32)],
        compiler_params=pltpu.CompilerParams(dimension_semantics=("arbitrary", "arbitrary"),
                                             vmem_limit_bytes=VMEM_LIMIT),
        name="in_proj",
    )(x, *consts)


def _attn_kernel(q_ref, k_ref, ncp_ref, vt_ref, km_ref, ncpm_ref, vtm_ref, o_ref, *, tq):
    hb = pl.program_id(1)
    seq = q_ref.shape[0]
    lane = lax.broadcasted_iota(jnp.int32, (tq, LANES), 1)
    key_row = lax.broadcasted_iota(jnp.int32, (tq, tq), 0)
    qry_col = lax.broadcasted_iota(jnp.int32, (tq, tq), 1)
    causal = key_row <= qry_col
    ones_m = jnp.ones((BF16_ROWS, N_META), BF16)
    ones_k = jnp.ones((BF16_ROWS, tq), BF16)
    kaug_m = jnp.concatenate([km_ref[...], ncpm_ref[...]], axis=1)
    head_masks, sels, vtm_aug = [], [], []
    for h in range(HEADS_PER_BLOCK):
        head_masks.append((lane >= h * HEAD_DIM) & (lane < (h + 1) * HEAD_DIM))
        hg = hb * HEADS_PER_BLOCK + h
        pick = (lane == hg) | (lane == hg + N_HEADS) | (lane == hg + 2 * N_HEADS)
        sels.append(jnp.where(pick, 1.0, 0.0).astype(BF16))
        vtm_aug.append(jnp.concatenate([vtm_ref[h * HEAD_DIM:(h + 1) * HEAD_DIM, :], ones_m], axis=0))

    def scores(qi):
        nk = (qi + 1) * tq
        qb = q_ref[qi * tq:(qi + 1) * tq, :]
        qaug = jnp.concatenate(
            [jnp.concatenate([jnp.where(head_masks[h], qb, jnp.zeros_like(qb)), sels[h]], axis=1)
             for h in range(HEADS_PER_BLOCK)], axis=0)
        kaug = jnp.concatenate(
            [kaug_m, jnp.concatenate([k_ref[0:nk, :], ncp_ref[0:nk, :]], axis=1)], axis=0)
        return lax.dot_general(kaug, qaug, NT_DIMS, preferred_element_type=F32)

    def finish(qi, s_all):
        outs = []
        for h in range(HEADS_PER_BLOCK):
            s = s_all[:, h * tq:(h + 1) * tq]
            sm = s[0:N_META]
            m = jnp.max(sm, axis=0, keepdims=True)
            acc = jnp.dot(vtm_aug[h], jnp.exp2(sm - m).astype(BF16), preferred_element_type=F32)
            for j in range(qi + 1):
                c = s[N_META + j * tq:N_META + (j + 1) * tq]
                if j == qi:
                    c = jnp.where(causal, c, -jnp.inf)
                m_new = jnp.maximum(m, jnp.max(c, axis=0, keepdims=True))
                p = jnp.exp2(c - m_new).astype(BF16)
                vta = jnp.concatenate([vt_ref[h * HEAD_DIM:(h + 1) * HEAD_DIM, j * tq:(j + 1) * tq], ones_k],
                                      axis=0)
                acc = acc * jnp.exp2(m - m_new) + jnp.dot(vta, p, preferred_element_type=F32)
                m = m_new
            outs.append(acc[0:HEAD_DIM] * (1.0 / acc[HEAD_DIM:HEAD_DIM + 1]))
        o_ref[qi * tq:(qi + 1) * tq, :] = jnp.concatenate(outs, axis=0).T.astype(o_ref.dtype)

    nq = seq // tq
    order = list(range(nq - 1, -1, -1))
    pending = [scores(qi) for qi in order[:SCORE_LOOKAHEAD]]
    for i, qi in enumerate(order):
        s_cur = pending.pop(0)
        if i + SCORE_LOOKAHEAD < nq:
            pending.append(scores(order[i + SCORE_LOOKAHEAD]))
        finish(qi, s_cur)


def _attention(q, k, ncp, vt, k_meta, ncp_meta, vt_meta, tq):
    b, s, _ = q.shape
    seq = lambda: pl.BlockSpec((None, s, LANES), lambda i, hb: (i, 0, hb))
    return pl.pallas_call(
        functools.partial(_attn_kernel, tq=tq),
        grid=(b, N_HEAD_BLOCKS),
        in_specs=[
            seq(), seq(),
            pl.BlockSpec((None, s, LANES), lambda i, hb: (i, 0, 0)),
            pl.BlockSpec((None, LANES, s), lambda i, hb: (i, hb, 0)),
            pl.BlockSpec((N_META, LANES), lambda i, hb: (0, hb)),
            pl.BlockSpec((N_META, LANES), lambda i, hb: (0, 0)),
            pl.BlockSpec((LANES, N_META), lambda i, hb: (hb, 0)),
        ],
        out_specs=seq(),
        out_shape=jax.ShapeDtypeStruct((b, s, D_ATTN), BF16),
        compiler_params=pltpu.CompilerParams(dimension_semantics=("arbitrary", "arbitrary"),
                                             vmem_limit_bytes=VMEM_LIMIT),
        name="fox_attention",
    )(q, k, ncp, vt, k_meta, ncp_meta, vt_meta)


def _zero_like_tile(v):
    bits = pltpu.bitcast(v, jnp.uint32)
    bits = lax.shift_right_logical(lax.shift_right_logical(bits, jnp.uint32(16)), jnp.uint32(16))
    return pltpu.bitcast(bits, F32)


def _shift_pass(win_ref, sh_ref, c, after):
    cs = slice(c * LANES, (c + 1) * LANES)
    win = win_ref[:, cs]
    if after is not None:
        win = win + _zero_like_tile(after)[0:1, :]
    n = win.shape[0]
    tok = None
    for shift in range(1, SUBLANES):
        rolled = pltpu.roll(win, n - shift, 0)
        sh_ref[shift - 1, :, cs] = rolled
        tok = rolled[0:SUBLANES]
    return tok


def _conv_unit(win_ref, sh_ref, w_ref, b_ref, y_ref, r0, c, after):
    off = CONV_HALO - (CONV_WIDTH - 1)
    cs = slice(c * LANES, (c + 1) * LANES)
    bias = b_ref[:, cs]
    if after is not None:
        bias = bias + _zero_like_tile(after)[0:1, :]
    acc = jnp.zeros((CONV_ROWS, LANES), F32) + bias
    for j in range(CONV_WIDTH):
        shift, base = (off + j) % SUBLANES, (off + j) // SUBLANES * SUBLANES
        rows = slice(r0 + base, r0 + base + CONV_ROWS)
        src = win_ref[rows, cs] if shift == 0 else sh_ref[shift - 1, rows, cs]
        acc = acc + src * w_ref[j:j + 1, cs]
    y_ref[r0:r0 + CONV_ROWS, cs] = acc
    return acc[CONV_ROWS - SUBLANES:CONV_ROWS, :]


def _back_kernel(x_ref, at_ref, u_ref, halo_ref, mhalo_ref, cw_ref, cb_ref, lg_ref, lb_ref, g_ref, wg_ref,
                 wao_ref, wco_ref, bco_ref, wout_ref, g2_ref, wup_ref, wdn_ref, gf_ref, o_ref,
                 win_ref, y_ref, cact_ref, sh_ref, *, tiles_per_row, n_tiles):
    tm = x_ref.shape[0]
    t = pl.program_id(0)

    @pl.when(t == 0)
    def _():
        cact_ref[...] = jnp.zeros_like(cact_ref)

    row_start = lax.rem(jnp.minimum(t, n_tiles - 1), tiles_per_row) == 0
    win_ref[0:CONV_HALO, :] = jnp.where(row_start, mhalo_ref[...], halo_ref[...])
    win_ref[CONV_HALO:CONV_HALO + tm, :] = u_ref[...]
    units = [("shift", c) for c in range(D_CONV // LANES)]
    units += [(r0, c) for c in range(D_CONV // LANES) for r0 in range(0, tm, CONV_ROWS)]

    def conv_group(n_units, after):
        tok = after
        for _ in range(n_units):
            r0, c = units.pop(0)
            if r0 == "shift":
                tok = _shift_pass(win_ref, sh_ref, c, tok)
            else:
                tok = _conv_unit(win_ref, sh_ref, cw_ref, cb_ref, y_ref, r0, c, tok)
        return tok

    last_tile = lambda v: v[v.shape[0] - SUBLANES:, v.shape[1] - LANES:]
    zeros_bf16 = lambda tok, shape: jnp.tile(
        jnp.concatenate([_zero_like_tile(tok)] * 2, axis=0).astype(BF16),
        (shape[0] // BF16_ROWS, shape[1] // LANES))
    slot = lax.rem(t, 2)
    cact_prev = cact_ref[1 - slot]

    def after(tok, v):
        return v if tok is None else v + zeros_bf16(tok, v.shape)

    x = x_ref[...]
    hn = _rms_norm(x, g_ref[...]).astype(BF16)
    ga_raw = jnp.dot(hn, wg_ref[:, 0:D_MODEL], preferred_element_type=F32)
    tok = conv_group(CONV_GROUPS[0], None)
    gc_raw = jnp.dot(after(tok, hn), wg_ref[:, D_MODEL:2 * D_MODEL], preferred_element_type=F32)
    tok = conv_group(CONV_GROUPS[1], last_tile(ga_raw))
    a = jnp.dot(after(tok, at_ref[...]), wao_ref[...], preferred_element_type=F32)
    tok = conv_group(CONV_GROUPS[2], last_tile(gc_raw))
    c = jnp.dot(after(tok, cact_prev), wco_ref[...], preferred_element_type=F32) + bco_ref[...]
    tok = conv_group(CONV_GROUPS[3], last_tile(a))
    mix = (_sigmoid(ga_raw) * a + _sigmoid(gc_raw) * c).astype(BF16)
    h1 = x + jnp.dot(after(tok, mix), wout_ref[...], preferred_element_type=F32)
    tok = conv_group(CONV_GROUPS[4], last_tile(c))
    hn2 = _rms_norm(h1, g2_ref[...]).astype(BF16)
    prev_result = last_tile(h1)
    h2 = h1
    n_pieces = 4
    piece = D_FF // n_pieces
    acts = []
    for i in range(n_pieces):
        up = jnp.dot(after(tok, hn2), wup_ref[:, i * piece:(i + 1) * piece], preferred_element_type=F32)
        tok = conv_group(CONV_GROUPS[5 + i], prev_result)
        prev_result = last_tile(up)
        acts.append(jnp.square(jnp.maximum(up, 0.0)).astype(BF16))
    for i in range(n_pieces):
        dn = jnp.dot(after(tok, acts[i]), wdn_ref[i * piece:(i + 1) * piece, :], preferred_element_type=F32)
        tok = conv_group(CONV_GROUPS[9 + i], prev_result)
        prev_result = last_tile(dn)
        h2 = h2 + dn
    assert not units
    y = y_ref[...]
    mu = jnp.mean(y, axis=-1, keepdims=True)
    yc = y - mu
    var = jnp.mean(yc * yc, axis=-1, keepdims=True)
    z = yc * lax.rsqrt(var + LN_EPS) * lg_ref[...] + lb_ref[...]
    cact_ref[slot] = (z * _sigmoid(z)).astype(BF16)
    o_ref[...] = _rms_norm(h2, gf_ref[...])


def _back(x, attn, u, meta_halo, consts, tm):
    b, s, d = x.shape
    nj = s // tm
    n_tiles = b * nj
    halo_blocks = tm // CONV_HALO
    prev = lambda t: jnp.maximum(t - 1, 0)
    cur = lambda t: jnp.minimum(t, n_tiles - 1)
    chain = lambda: pl.BlockSpec((None, tm, d), lambda t: (prev(t) // nj, prev(t) % nj, 0))
    return pl.pallas_call(
        functools.partial(_back_kernel, tiles_per_row=nj, n_tiles=n_tiles),
        grid=(n_tiles + 1,),
        in_specs=[chain(), chain(),
                  pl.BlockSpec((None, tm, d), lambda t: (cur(t) // nj, cur(t) % nj, 0)),
                  pl.BlockSpec((None, CONV_HALO, d),
                               lambda t: (cur(t) // nj, jnp.maximum(cur(t) % nj * halo_blocks - 1, 0), 0)),
                  _const_spec(meta_halo.shape)] + [_const_spec(c.shape) for c in consts],
        out_specs=chain(),
        out_shape=jax.ShapeDtypeStruct((b, s, d), F32),
        scratch_shapes=[pltpu.VMEM((CONV_HALO + tm, d), F32), pltpu.VMEM((tm, d), F32),
                        pltpu.VMEM((2, tm, d), BF16), pltpu.VMEM((SUBLANES - 1, CONV_HALO + tm, d), F32)],
        compiler_params=pltpu.CompilerParams(dimension_semantics=("arbitrary",),
                                             vmem_limit_bytes=VMEM_LIMIT),
        name="conv_merge_mlp",
    )(x, attn, u, u, meta_halo, *consts)


def kernel(x, meta_tokens, norm_mix_gain, w_in, b_forget, w_attn_out, b_glu, conv_dw_w, conv_dw_b,
           conv_ln_gain, conv_ln_bias, w_conv_out, b_conv_out, w_out, norm_mlp_gain, w_mlp_up,
           w_mlp_down, final_norm_gain):
    assert w_in.shape[0] == 1, "single layer"
    o_v, o_fg, o_glu = 2 * D_ATTN, 3 * D_ATTN, 3 * D_ATTN + N_HEADS
    o_gate = o_glu + 2 * D_CONV
    w = w_in[0].astype(BF16)
    wqk = w[:, 0:o_v]
    wvt = w[:, o_v:o_fg].T
    lane_pad = LANES - N_SPLIT * N_HEADS
    wfg = jnp.pad(jnp.tile(w[:, o_fg:o_glu], (1, N_SPLIT)), ((0, 0), (0, lane_pad)))
    bfg = jnp.pad(jnp.tile(b_forget[0], N_SPLIT), (0, lane_pad))[None, :]
    wglu = w[:, o_glu:o_gate]
    wgate = w[:, o_gate:]
    g_mix = norm_mix_gain[0][None, :]
    bglu = b_glu[0][None, :]

    proj = functools.partial(_in_proj, g=g_mix, wqk=wqk, wvt=wvt, wfg=wfg, bfg=bfg, wglu=wglu, bglu=bglu)
    _, k_m, vt_m, ncp_m, u_m, carry_m = proj(meta_tokens[None].astype(x.dtype), jnp.zeros((1, LANES), F32),
                                              tm=N_META)
    q, k, vt, ncp, u, _ = proj(x, carry_m[0], tm=512)

    attn = _attention(q, k, ncp, vt, k_m[0], ncp_m[0], vt_m[0], tq=256)

    meta_halo = jnp.concatenate([jnp.zeros((CONV_HALO - N_META, D_CONV), F32), u_m[0]], axis=0)
    vec = lambda v: v[None, :]
    consts = (conv_dw_w[0], vec(conv_dw_b[0]), vec(conv_ln_gain[0]), vec(conv_ln_bias[0]), g_mix, wgate,
              w_attn_out[0].astype(BF16), w_conv_out[0].astype(BF16), vec(b_conv_out[0]), w_out[0].astype(BF16),
              vec(norm_mlp_gain[0]), w_mlp_up[0].astype(BF16), w_mlp_down[0].astype(BF16), vec(final_norm_gain))
    return _back(x, attn, u, meta_halo, consts, tm=256)
```

```python
import functools

import jax
import jax.numpy as jnp
from jax import lax
from jax.experimental import pallas as pl
from jax.experimental.pallas import tpu as pltpu

D_MODEL = 1024
N_META = 16
N_HEADS = 16
HEAD_DIM = 64
D_ATTN = N_HEADS * HEAD_DIM
D_CONV = D_MODEL
CONV_WIDTH = 31
D_FF = 4 * D_MODEL
RMS_EPS = 1e-6
LN_EPS = 1e-5
LOG2E = 1.4426950408889634

LANES = 128
SUBLANES = 8
BF16_ROWS = 16
HEADS_PER_BLOCK = LANES // HEAD_DIM
N_HEAD_BLOCKS = N_HEADS // HEADS_PER_BLOCK
SCORE_LOOKAHEAD = 2
N_SPLIT = 3
CONV_HALO = 32
CONV_ROWS = 128
CONV_GROUPS = (10, 2, 2, 1, 1, 1, 1, 1, 1, 1, 1, 1, 1)
VMEM_LIMIT = 56 * 1024 * 1024

BF16 = jnp.bfloat16
F32 = jnp.float32
NT_DIMS = (((1,), (1,)), ((), ()))


def _const_spec(shape):
    nd = len(shape)
    return pl.BlockSpec(shape, lambda *_: (0,) * nd, pipeline_mode=pl.Buffered(1))


def _sigmoid(x):
    return 1.0 / (1.0 + jnp.exp(-x))


def _rms_norm(x, g):
    ms = jnp.mean(x * x, axis=-1, keepdims=True)
    return x * lax.rsqrt(ms + RMS_EPS) * g


def _log_sigmoid(x):
    return jnp.minimum(x, 0.0) - jnp.log(1.0 + jnp.exp(-jnp.abs(x)))


def _split3_by_lane_group(x, lane):
    hi = x.astype(BF16).astype(F32)
    r1 = x - hi
    mid = r1.astype(BF16).astype(F32)
    lo = (r1 - mid).astype(BF16).astype(F32)
    out = jnp.where(lane < N_HEADS, hi,
                    jnp.where(lane < 2 * N_HEADS, mid, jnp.where(lane < 3 * N_HEADS, lo, 0.0)))
    return out.astype(BF16)


def _in_proj_kernel(x_ref, g_ref, wqk_ref, wvt_ref, wfg_ref, bfg_ref, wglu_ref, bglu_ref, tril_ref, carry0_ref,
                    q_ref, k_ref, vt_ref, ncp_ref, u_ref, carry_out_ref, carry_ref):
    tm = x_ref.shape[0]
    hn = _rms_norm(x_ref[...], g_ref[...]).astype(BF16)
    lane = lax.broadcasted_iota(jnp.int32, (tm, LANES), 1)
    fg = jnp.dot(hn, wfg_ref[...], preferred_element_type=F32) + bfg_ref[...]
    parts = _split3_by_lane_group(_log_sigmoid(fg), lane)

    a = jnp.dot(hn, wglu_ref[:, 0:D_CONV], preferred_element_type=F32) + bglu_ref[:, 0:D_CONV]
    gl = jnp.dot(hn, wglu_ref[:, D_CONV:2 * D_CONV], preferred_element_type=F32) + bglu_ref[:, D_CONV:2 * D_CONV]
    u_ref[...] = a * _sigmoid(gl)

    scale = HEAD_DIM ** -0.5 * LOG2E
    q = jnp.dot(hn, wqk_ref[:, 0:D_ATTN], preferred_element_type=F32)
    q_ref[...] = (q * scale).astype(BF16)
    k = jnp.dot(hn, wqk_ref[:, D_ATTN:2 * D_ATTN], preferred_element_type=F32)
    k_ref[...] = k.astype(BF16)
    cum3 = jnp.dot(tril_ref[...], parts, preferred_element_type=F32)
    vt = lax.dot_general(wvt_ref[...], hn, NT_DIMS, preferred_element_type=F32)
    vt_ref[...] = vt.astype(BF16)

    tot = cum3 + pltpu.roll(cum3, LANES - N_HEADS, 1) + pltpu.roll(cum3, LANES - 2 * N_HEADS, 1)
    tot3 = jnp.where(lane < N_HEADS, tot,
                     jnp.where(lane < 2 * N_HEADS, pltpu.roll(tot, N_HEADS, 1), pltpu.roll(tot, 2 * N_HEADS, 1)))

    @pl.when(pl.program_id(1) == 0)
    def _():
        carry_ref[...] = carry0_ref[...]

    cum = tot3 + carry_ref[...]
    last = cum[tm - 1:tm, :]
    carry_ref[...] = last
    carry_out_ref[...] = last
    ncp_ref[...] = _split3_by_lane_group(-LOG2E * cum, lane)


def _in_proj(x, carry0, g, wqk, wvt, wfg, bfg, wglu, bglu, tm):
    b, s, d = x.shape
    tril = jnp.tril(jnp.ones((tm, tm), F32)).astype(BF16)
    row = lambda w: pl.BlockSpec((None, tm, w), lambda i, j: (i, j, 0))
    consts = (g, wqk, wvt, wfg, bfg, wglu, bglu, tril, carry0)
    return pl.pallas_call(
        _in_proj_kernel,
        grid=(b, s // tm),
        in_specs=[row(d)] + [_const_spec(c.shape) for c in consts],
        out_specs=[row(D_ATTN), row(D_ATTN),
                   pl.BlockSpec((None, D_ATTN, tm), lambda i, j: (i, 0, j)),
                   row(LANES), row(D_CONV),
                   pl.BlockSpec((None, 1, LANES), lambda i, j: (i, 0, 0))],
        out_shape=[jax.ShapeDtypeStruct((b, s, D_ATTN), BF16), jax.ShapeDtypeStruct((b, s, D_ATTN), BF16),
                   jax.ShapeDtypeStruct((b, D_ATTN, s), BF16), jax.ShapeDtypeStruct((b, s, LANES), BF16),
                   jax.ShapeDtypeStruct((b, s, D_CONV), F32), jax.ShapeDtypeStruct((b, 1, LANES), F32)],
        scratch_shapes=[pltpu.VMEM((1, LANES), F32)],
        compiler_params=pltpu.CompilerParams(dimension_semantics=("arbitrary", "arbitrary"),
                                             vmem_limit_bytes=VMEM_LIMIT),
        name="in_proj",
    )(x, *consts)


def _attn_kernel(q_ref, k_ref, ncp_ref, vt_ref, km_ref, ncpm_ref, vtm_ref, o_ref, *, tq):
    hb = pl.program_id(1)
    seq = q_ref.shape[0]
    lane = lax.broadcasted_iota(jnp.int32, (tq, LANES), 1)
    key_row = lax.broadcasted_iota(jnp.int32, (tq, tq), 0)
    qry_col = lax.broadcasted_iota(jnp.int32, (tq, tq), 1)
    causal = key_row <= qry_col
    ones_m = jnp.ones((BF16_ROWS, N_META), BF16)
    ones_k = jnp.ones((BF16_ROWS, tq), BF16)
    kaug_m = jnp.concatenate([km_ref[...], ncpm_ref[...]], axis=1)
    head_masks, sels, vtm_aug = [], [], []
    for h in range(HEADS_PER_BLOCK):
        head_masks.append((lane >= h * HEAD_DIM) & (lane < (h + 1) * HEAD_DIM))
        hg = hb * HEADS_PER_BLOCK + h
        pick = (lane == hg) | (lane == hg + N_HEADS) | (lane == hg + 2 * N_HEADS)
        sels.append(jnp.where(pick, 1.0, 0.0).astype(BF16))
        vtm_aug.append(jnp.concatenate([vtm_ref[h * HEAD_DIM:(h + 1) * HEAD_DIM, :], ones_m], axis=0))

    def scores(qi):
        nk = (qi + 1) * tq
        qb = q_ref[qi * tq:(qi + 1) * tq, :]
        qaug = jnp.concatenate(
            [jnp.concatenate([jnp.where(head_masks[h], qb, jnp.zeros_like(qb)), sels[h]], axis=1)
             for h in range(HEADS_PER_BLOCK)], axis=0)
        kaug = jnp.concatenate(
            [kaug_m, jnp.concatenate([k_ref[0:nk, :], ncp_ref[0:nk, :]], axis=1)], axis=0)
        return lax.dot_general(kaug, qaug, NT_DIMS, preferred_element_type=F32)

    def finish(qi, s_all):
        outs = []
        for h in range(HEADS_PER_BLOCK):
            s = s_all[:, h * tq:(h + 1) * tq]
            sm = s[0:N_META]
            m = jnp.max(sm, axis=0, keepdims=True)
            acc = jnp.dot(vtm_aug[h], jnp.exp2(sm - m).astype(BF16), preferred_element_type=F32)
            for j in range(qi + 1):
                c = s[N_META + j * tq:N_META + (j + 1) * tq]
                if j == qi:
                    c = jnp.where(causal, c, -jnp.inf)
                m_new = jnp.maximum(m, jnp.max(c, axis=0, keepdims=True))
                p = jnp.exp2(c - m_new).astype(BF16)
                vta = jnp.concatenate([vt_ref[h * HEAD_DIM:(h + 1) * HEAD_DIM, j * tq:(j + 1) * tq], ones_k],
                                      axis=0)
                acc = acc * jnp.exp2(m - m_new) + jnp.dot(vta, p, preferred_element_type=F32)
                m = m_new
            outs.append(acc[0:HEAD_DIM] * (1.0 / acc[HEAD_DIM:HEAD_DIM + 1]))
        o_ref[qi * tq:(qi + 1) * tq, :] = jnp.concatenate(outs, axis=0).T.astype(o_ref.dtype)

    nq = seq // tq
    order = list(range(nq - 1, -1, -1))
    pending = [scores(qi) for qi in order[:SCORE_LOOKAHEAD]]
    for i, qi in enumerate(order):
        s_cur = pending.pop(0)
        if i + SCORE_LOOKAHEAD < nq:
            pending.append(scores(order[i + SCORE_LOOKAHEAD]))
        finish(qi, s_cur)


def _attention(q, k, ncp, vt, k_meta, ncp_meta, vt_meta, tq):
    b, s, _ = q.shape
    seq = lambda: pl.BlockSpec((None, s, LANES), lambda i, hb: (i, 0, hb))
    return pl.pallas_call(
        functools.partial(_attn_kernel, tq=tq),
        grid=(b, N_HEAD_BLOCKS),
        in_specs=[
            seq(), seq(),
            pl.BlockSpec((None, s, LANES), lambda i, hb: (i, 0, 0)),
            pl.BlockSpec((None, LANES, s), lambda i, hb: (i, hb, 0)),
            pl.BlockSpec((N_META, LANES), lambda i, hb: (0, hb)),
            pl.BlockSpec((N_META, LANES), lambda i, hb: (0, 0)),
            pl.BlockSpec((LANES, N_META), lambda i, hb: (hb, 0)),
        ],
        out_specs=seq(),
        out_shape=jax.ShapeDtypeStruct((b, s, D_ATTN), BF16),
        compiler_params=pltpu.CompilerParams(dimension_semantics=("arbitrary", "arbitrary"),
                                             vmem_limit_bytes=VMEM_LIMIT),
        name="fox_attention",
    )(q, k, ncp, vt, k_meta, ncp_meta, vt_meta)


def _zero_like_tile(v):
    bits = pltpu.bitcast(v, jnp.uint32)
    bits = lax.shift_right_logical(lax.shift_right_logical(bits, jnp.uint32(16)), jnp.uint32(16))
    return pltpu.bitcast(bits, F32)


def _shift_pass(win_ref, sh_ref, c, after):
    cs = slice(c * LANES, (c + 1) * LANES)
    win = win_ref[:, cs]
    if after is not None:
        win = win + _zero_like_tile(after)[0:1, :]
    n = win.shape[0]
    tok = None
    for shift in range(1, SUBLANES):
        rolled = pltpu.roll(win, n - shift, 0)
        sh_ref[shift - 1, :, cs] = rolled
        tok = rolled[0:SUBLANES]
    return tok


def _conv_unit(win_ref, sh_ref, w_ref, b_ref, y_ref, r0, c, after):
    off = CONV_HALO - (CONV_WIDTH - 1)
    cs = slice(c * LANES, (c + 1) * LANES)
    bias = b_ref[:, cs]
    if after is not None:
        bias = bias + _zero_like_tile(after)[0:1, :]
    acc = jnp.zeros((CONV_ROWS, LANES), F32) + bias
    for j in range(CONV_WIDTH):
        shift, base = (off + j) % SUBLANES, (off + j) // SUBLANES * SUBLANES
        rows = slice(r0 + base, r0 + base + CONV_ROWS)
        src = win_ref[rows, cs] if shift == 0 else sh_ref[shift - 1, rows, cs]
        acc = acc + src * w_ref[j:j + 1, cs]
    y_ref[r0:r0 + CONV_ROWS, cs] = acc
    return acc[CONV_ROWS - SUBLANES:CONV_ROWS, :]


def _back_kernel(x_ref, at_ref, u_ref, halo_ref, mhalo_ref, cw_ref, cb_ref, lg_ref, lb_ref, g_ref, wg_ref,
                 wao_ref, wco_ref, bco_ref, wout_ref, g2_ref, wup_ref, wdn_ref, gf_ref, o_ref,
                 win_ref, y_ref, cact_ref, sh_ref, *, tiles_per_row, n_tiles):
    tm = x_ref.shape[0]
    t = pl.program_id(0)

    @pl.when(t == 0)
    def _():
        cact_ref[...] = jnp.zeros_like(cact_ref)

    row_start = lax.rem(jnp.minimum(t, n_tiles - 1), tiles_per_row) == 0
    win_ref[0:CONV_HALO, :] = jnp.where(row_start, mhalo_ref[...], halo_ref[...])
    win_ref[CONV_HALO:CONV_HALO + tm, :] = u_ref[...]
    units = [("shift", c) for c in range(D_CONV // LANES)]
    units += [(r0, c) for c in range(D_CONV // LANES) for r0 in range(0, tm, CONV_ROWS)]

    def conv_group(n_units, after):
        tok = after
        for _ in range(n_units):
            r0, c = units.pop(0)
            if r0 == "shift":
                tok = _shift_pass(win_ref, sh_ref, c, tok)
            else:
                tok = _conv_unit(win_ref, sh_ref, cw_ref, cb_ref, y_ref, r0, c, tok)
        return tok

    last_tile = lambda v: v[v.shape[0] - SUBLANES:, v.shape[1] - LANES:]
    zeros_bf16 = lambda tok, shape: jnp.tile(
        jnp.concatenate([_zero_like_tile(tok)] * 2, axis=0).astype(BF16),
        (shape[0] // BF16_ROWS, shape[1] // LANES))
    slot = lax.rem(t, 2)
    cact_prev = cact_ref[1 - slot]

    def after(tok, v):
        return v if tok is None else v + zeros_bf16(tok, v.shape)

    x = x_ref[...]
    hn = _rms_norm(x, g_ref[...]).astype(BF16)
    ga_raw = jnp.dot(hn, wg_ref[:, 0:D_MODEL], preferred_element_type=F32)
    tok = conv_group(CONV_GROUPS[0], None)
    gc_raw = jnp.dot(after(tok, hn), wg_ref[:, D_MODEL:2 * D_MODEL], preferred_element_type=F32)
    tok = conv_group(CONV_GROUPS[1], last_tile(ga_raw))
    a = jnp.dot(after(tok, at_ref[...]), wao_ref[...], preferred_element_type=F32)
    tok = conv_group(CONV_GROUPS[2], last_tile(gc_raw))
    c = jnp.dot(after(tok, cact_prev), wco_ref[...], preferred_element_type=F32) + bco_ref[...]
    tok = conv_group(CONV_GROUPS[3], last_tile(a))
    mix = (_sigmoid(ga_raw) * a + _sigmoid(gc_raw) * c).astype(BF16)
    h1 = x + jnp.dot(after(tok, mix), wout_ref[...], preferred_element_type=F32)
    tok = conv_group(CONV_GROUPS[4], last_tile(c))
    hn2 = _rms_norm(h1, g2_ref[...]).astype(BF16)
    prev_result = last_tile(h1)
    h2 = h1
    n_pieces = 4
    piece = D_FF // n_pieces
    acts = []
    for i in range(n_pieces):
        up = jnp.dot(after(tok, hn2), wup_ref[:, i * piece:(i + 1) * piece], preferred_element_type=F32)
        tok = conv_group(CONV_GROUPS[5 + i], prev_result)
        prev_result = last_tile(up)
        acts.append(jnp.square(jnp.maximum(up, 0.0)).astype(BF16))
    for i in range(n_pieces):
        dn = jnp.dot(after(tok, acts[i]), wdn_ref[i * piece:(i + 1) * piece, :], preferred_element_type=F32)
        tok = conv_group(CONV_GROUPS[9 + i], prev_result)
        prev_result = last_tile(dn)
        h2 = h2 + dn
    assert not units
    y = y_ref[...]
    mu = jnp.mean(y, axis=-1, keepdims=True)
    yc = y - mu
    var = jnp.mean(yc * yc, axis=-1, keepdims=True)
    z = yc * lax.rsqrt(var + LN_EPS) * lg_ref[...] + lb_ref[...]
    cact_ref[slot] = (z * _sigmoid(z)).astype(BF16)
    o_ref[...] = _rms_norm(h2, gf_ref[...])


def _back(x, attn, u, meta_halo, consts, tm):
    b, s, d = x.shape
    nj = s // tm
    n_tiles = b * nj
    halo_blocks = tm // CONV_HALO
    prev = lambda t: jnp.maximum(t - 1, 0)
    cur = lambda t: jnp.minimum(t, n_tiles - 1)
    chain = lambda: pl.BlockSpec((None, tm, d), lambda t: (prev(t) // nj, prev(t) % nj, 0))
    return pl.pallas_call(
        functools.partial(_back_kernel, tiles_per_row=nj, n_tiles=n_tiles),
        grid=(n_tiles + 1,),
        in_specs=[chain(), chain(),
                  pl.BlockSpec((None, tm, d), lambda t: (cur(t) // nj, cur(t) % nj, 0)),
                  pl.BlockSpec((None, CONV_HALO, d),
                               lambda t: (cur(t) // nj, jnp.maximum(cur(t) % nj * halo_blocks - 1, 0), 0)),
                  _const_spec(meta_halo.shape)] + [_const_spec(c.shape) for c in consts],
        out_specs=chain(),
        out_shape=jax.ShapeDtypeStruct((b, s, d), F32),
        scratch_shapes=[pltpu.VMEM((CONV_HALO + tm, d), F32), pltpu.VMEM((tm, d), F32),
                        pltpu.VMEM((2, tm, d), BF16), pltpu.VMEM((SUBLANES - 1, CONV_HALO + tm, d), F32)],
        compiler_params=pltpu.CompilerParams(dimension_semantics=("arbitrary",),
                                             vmem_limit_bytes=VMEM_LIMIT),
        name="conv_merge_mlp",
    )(x, attn, u, u, meta_halo, *consts)


def kernel(x, meta_tokens, norm_mix_gain, w_in, b_forget, w_attn_out, b_glu, conv_dw_w, conv_dw_b,
           conv_ln_gain, conv_ln_bias, w_conv_out, b_conv_out, w_out, norm_mlp_gain, w_mlp_up,
           w_mlp_down, final_norm_gain):
    assert w_in.shape[0] == 1, "single layer"
    o_v, o_fg, o_glu = 2 * D_ATTN, 3 * D_ATTN, 3 * D_ATTN + N_HEADS
    o_gate = o_glu + 2 * D_CONV
    w = w_in[0].astype(BF16)
    wqk = w[:, 0:o_v]
    wvt = w[:, o_v:o_fg].T
    lane_pad = LANES - N_SPLIT * N_HEADS
    wfg = jnp.pad(jnp.tile(w[:, o_fg:o_glu], (1, N_SPLIT)), ((0, 0), (0, lane_pad)))
    bfg = jnp.pad(jnp.tile(b_forget[0], N_SPLIT), (0, lane_pad))[None, :]
    wglu = w[:, o_glu:o_gate]
    wgate = w[:, o_gate:]
    g_mix = norm_mix_gain[0][None, :]
    bglu = b_glu[0][None, :]

    proj = functools.partial(_in_proj, g=g_mix, wqk=wqk, wvt=wvt, wfg=wfg, bfg=bfg, wglu=wglu, bglu=bglu)
    _, k_m, vt_m, ncp_m, u_m, carry_m = proj(meta_tokens[None].astype(x.dtype), jnp.zeros((1, LANES), F32),
                                              tm=N_META)
    q, k, vt, ncp, u, _ = proj(x, carry_m[0], tm=512)

    attn = _attention(q, k, ncp, vt, k_m[0], ncp_m[0], vt_m[0], tq=256)

    meta_halo = jnp.concatenate([jnp.zeros((CONV_HALO - N_META, D_CONV), F32), u_m[0]], axis=0)
    vec = lambda v: v[None, :]
    consts = (conv_dw_w[0], vec(conv_dw_b[0]), vec(conv_ln_gain[0]), vec(conv_ln_bias[0]), g_mix, wgate,
              w_attn_out[0].astype(BF16), w_conv_out[0].astype(BF16), vec(b_conv_out[0]), w_out[0].astype(BF16),
              vec(norm_mlp_gain[0]), w_mlp_up[0].astype(BF16), w_mlp_down[0].astype(BF16), vec(final_norm_gain))
    return _back(x, attn, u, meta_halo, consts, tm=256)
```

```python
import functools

import jax
import jax.numpy as jnp
from jax import lax
from jax.experimental import pallas as pl
from jax.experimental.pallas import tpu as pltpu

D_MODEL = 1024
N_META = 16
N_HEADS = 16
HEAD_DIM = 64
D_ATTN = N_HEADS * HEAD_DIM
D_CONV = D_MODEL
CONV_WIDTH = 31
D_FF = 4 * D_MODEL
RMS_EPS = 1e-6
LN_EPS = 1e-5
LOG2E = 1.4426950408889634

LANES = 128
SUBLANES = 8
BF16_ROWS = 16
HEADS_PER_BLOCK = LANES // HEAD_DIM
N_HEAD_BLOCKS = N_HEADS // HEADS_PER_BLOCK
SCORE_LOOKAHEAD = 2
N_SPLIT = 3
CONV_HALO = 32
CONV_ROWS = 128
CONV_GROUPS = (10, 2, 2, 1, 1, 1, 1, 1, 1, 1, 1, 1, 1)
VMEM_LIMIT = 56 * 1024 * 1024
IN_PROJ_ROWS = 512
ATTN_QUERY_BLOCK = 256
BACK_ROWS = 256

BF16 = jnp.bfloat16
F32 = jnp.float32
NT_DIMS = (((1,), (1,)), ((), ()))


def _const_spec(shape):
    nd = len(shape)
    return pl.BlockSpec(shape, lambda *_: (0,) * nd, pipeline_mode=pl.Buffered(1))


def _sigmoid(x):
    return 1.0 / (1.0 + jnp.exp(-x))


def _rms_norm(x, g):
    ms = jnp.mean(x * x, axis=-1, keepdims=True)
    return x * lax.rsqrt(ms + RMS_EPS) * g


def _log_sigmoid(x):
    return jnp.minimum(x, 0.0) - jnp.log(1.0 + jnp.exp(-jnp.abs(x)))


def _split3_by_lane_group(x, lane):
    hi = x.astype(BF16).astype(F32)
    r1 = x - hi
    mid = r1.astype(BF16).astype(F32)
    lo = (r1 - mid).astype(BF16).astype(F32)
    out = jnp.where(lane < N_HEADS, hi,
                    jnp.where(lane < 2 * N_HEADS, mid, jnp.where(lane < 3 * N_HEADS, lo, 0.0)))
    return out.astype(BF16)


def _in_proj_kernel(x_ref, g_ref, wqk_ref, wvt_ref, wfg_ref, bfg_ref, wglu_ref, bglu_ref, tril_ref, carry0_ref,
                    q_ref, k_ref, vt_ref, ncp_ref, u_ref, carry_out_ref, carry_ref):
    tm = x_ref.shape[0]

    @pl.when(pl.program_id(1) == 0)
    def _():
        carry_ref[...] = carry0_ref[...]

    hn = _rms_norm(x_ref[...], g_ref[...]).astype(BF16)
    lane = lax.broadcasted_iota(jnp.int32, (tm, LANES), 1)
    fg = jnp.dot(hn, wfg_ref[...], preferred_element_type=F32) + bfg_ref[...]
    parts = _split3_by_lane_group(_log_sigmoid(fg), lane)

    a = jnp.dot(hn, wglu_ref[:, 0:D_CONV], preferred_element_type=F32) + bglu_ref[:, 0:D_CONV]
    gl = jnp.dot(hn, wglu_ref[:, D_CONV:2 * D_CONV], preferred_element_type=F32) + bglu_ref[:, D_CONV:2 * D_CONV]
    u_ref[...] = a * _sigmoid(gl)

    scale = HEAD_DIM ** -0.5 * LOG2E
    q = jnp.dot(hn, wqk_ref[:, 0:D_ATTN], preferred_element_type=F32)
    q_ref[...] = (q * scale).astype(BF16)
    k = jnp.dot(hn, wqk_ref[:, D_ATTN:2 * D_ATTN], preferred_element_type=F32)
    k_ref[...] = k.astype(BF16)
    cum3 = jnp.dot(tril_ref[...], parts, preferred_element_type=F32)
    vt = lax.dot_general(wvt_ref[...], hn, NT_DIMS, preferred_element_type=F32)
    vt_ref[...] = vt.astype(BF16)

    tot = cum3 + pltpu.roll(cum3, LANES - N_HEADS, 1) + pltpu.roll(cum3, LANES - 2 * N_HEADS, 1)
    tot3 = jnp.where(lane < N_HEADS, tot,
                     jnp.where(lane < 2 * N_HEADS, pltpu.roll(tot, N_HEADS, 1), pltpu.roll(tot, 2 * N_HEADS, 1)))

    cum = tot3 + carry_ref[...]
    last = cum[tm - 1:tm, :]
    carry_ref[...] = last
    carry_out_ref[...] = last
    ncp_ref[...] = _split3_by_lane_group(-LOG2E * cum, lane)


def _in_proj(x, carry0, g, wqk, wvt, wfg, bfg, wglu, bglu, tm):
    b, s, d = x.shape
    tril = jnp.tril(jnp.ones((tm, tm), F32)).astype(BF16)
    row = lambda w: pl.BlockSpec((None, tm, w), lambda i, j: (i, j, 0))
    consts = (g, wqk, wvt, wfg, bfg, wglu, bglu, tril, carry0)
    return pl.pallas_call(
        _in_proj_kernel,
        grid=(b, s // tm),
        in_specs=[row(d)] + [_const_spec(c.shape) for c in consts],
        out_specs=[row(D_ATTN), row(D_ATTN),
                   pl.BlockSpec((None, D_ATTN, tm), lambda i, j: (i, 0, j)),
                   row(LANES), row(D_CONV),
                   pl.BlockSpec((None, 1, LANES), lambda i, j: (i, 0, 0))],
        out_shape=[jax.ShapeDtypeStruct((b, s, D_ATTN), BF16), jax.ShapeDtypeStruct((b, s, D_ATTN), BF16),
                   jax.ShapeDtypeStruct((b, D_ATTN, s), BF16), jax.ShapeDtypeStruct((b, s, LANES), BF16),
                   jax.ShapeDtypeStruct((b, s, D_CONV), F32), jax.ShapeDtypeStruct((b, 1, LANES), F32)],
        scratch_shapes=[pltpu.VMEM((1, LANES), F32)],
        compiler_params=pltpu.CompilerParams(dimension_semantics=("arbitrary", "arbitrary"),
                                             vmem_limit_bytes=VMEM_LIMIT),
        name="in_proj",
    )(x, *consts)


def _attn_kernel(q_ref, k_ref, ncp_ref, vt_ref, km_ref, ncpm_ref, vtm_ref, o_ref, *, tq):
    hb = pl.program_id(1)
    seq = q_ref.shape[0]
    lane = lax.broadcasted_iota(jnp.int32, (tq, LANES), 1)
    key_row = lax.broadcasted_iota(jnp.int32, (tq, tq), 0)
    qry_col = lax.broadcasted_iota(jnp.int32, (tq, tq), 1)
    causal = key_row <= qry_col
    ones_m = jnp.ones((BF16_ROWS, N_META), BF16)
    ones_k = jnp.ones((BF16_ROWS, tq), BF16)
    kaug_m = jnp.concatenate([km_ref[...], ncpm_ref[...]], axis=1)
    head_masks, sels, vtm_aug = [], [], []
    for h in range(HEADS_PER_BLOCK):
        head_masks.append((lane >= h * HEAD_DIM) & (lane < (h + 1) * HEAD_DIM))
        hg = hb * HEADS_PER_BLOCK + h
        pick = (lane == hg) | (lane == hg + N_HEADS) | (lane == hg + 2 * N_HEADS)
        sels.append(jnp.where(pick, 1.0, 0.0).astype(BF16))
        vtm_aug.append(jnp.concatenate([vtm_ref[h * HEAD_DIM:(h + 1) * HEAD_DIM, :], ones_m], axis=0))

    def scores(qi):
        nk = (qi + 1) * tq
        qb = q_ref[qi * tq:(qi + 1) * tq, :]
        qaug = jnp.concatenate(
            [jnp.concatenate([jnp.where(head_masks[h], qb, jnp.zeros_like(qb)), sels[h]], axis=1)
             for h in range(HEADS_PER_BLOCK)], axis=0)
        kaug = jnp.concatenate(
            [kaug_m, jnp.concatenate([k_ref[0:nk, :], ncp_ref[0:nk, :]], axis=1)], axis=0)
        return lax.dot_general(kaug, qaug, NT_DIMS, preferred_element_type=F32)

    def finish(qi, s_all):
        outs = []
        for h in range(HEADS_PER_BLOCK):
            s = s_all[:, h * tq:(h + 1) * tq]
            sm = s[0:N_META]
            m = jnp.max(sm, axis=0, keepdims=True)
            acc = jnp.dot(vtm_aug[h], jnp.exp2(sm - m).astype(BF16), preferred_element_type=F32)
            for j in range(qi + 1):
                c = s[N_META + j * tq:N_META + (j + 1) * tq]
                if j == qi:
                    c = jnp.where(causal, c, -jnp.inf)
                m_new = jnp.maximum(m, jnp.max(c, axis=0, keepdims=True))
                p = jnp.exp2(c - m_new).astype(BF16)
                vta = jnp.concatenate([vt_ref[h * HEAD_DIM:(h + 1) * HEAD_DIM, j * tq:(j + 1) * tq], ones_k],
                                      axis=0)
                acc = acc * jnp.exp2(m - m_new) + jnp.dot(vta, p, preferred_element_type=F32)
                m = m_new
            outs.append(acc[0:HEAD_DIM] * (1.0 / acc[HEAD_DIM:HEAD_DIM + 1]))
        o_ref[qi * tq:(qi + 1) * tq, :] = jnp.concatenate(outs, axis=0).T.astype(o_ref.dtype)

    nq = seq // tq
    order = list(range(nq - 1, -1, -1))
    pending = [scores(qi) for qi in order[:SCORE_LOOKAHEAD]]
    for i, qi in enumerate(order):
        s_cur = pending.pop(0)
        if i + SCORE_LOOKAHEAD < nq:
            pending.append(scores(order[i + SCORE_LOOKAHEAD]))
        finish(qi, s_cur)


def _attention(q, k, ncp, vt, k_meta, ncp_meta, vt_meta, tq):
    b, s, _ = q.shape
    seq = lambda: pl.BlockSpec((None, s, LANES), lambda i, hb: (i, 0, hb))
    return pl.pallas_call(
        functools.partial(_attn_kernel, tq=tq),
        grid=(b, N_HEAD_BLOCKS),
        in_specs=[
            seq(), seq(),
            pl.BlockSpec((None, s, LANES), lambda i, hb: (i, 0, 0)),
            pl.BlockSpec((None, LANES, s), lambda i, hb: (i, hb, 0)),
            pl.BlockSpec((N_META, LANES), lambda i, hb: (0, hb)),
            pl.BlockSpec((N_META, LANES), lambda i, hb: (0, 0)),
            pl.BlockSpec((LANES, N_META), lambda i, hb: (hb, 0)),
        ],
        out_specs=seq(),
        out_shape=jax.ShapeDtypeStruct((b, s, D_ATTN), BF16),
        compiler_params=pltpu.CompilerParams(dimension_semantics=("arbitrary", "arbitrary"),
                                             vmem_limit_bytes=VMEM_LIMIT),
        name="fox_attention",
    )(q, k, ncp, vt, k_meta, ncp_meta, vt_meta)


def _zero_like_tile(v):
    bits = pltpu.bitcast(v, jnp.uint32)
    bits = lax.shift_right_logical(lax.shift_right_logical(bits, jnp.uint32(16)), jnp.uint32(16))
    return pltpu.bitcast(bits, F32)


def _shift_pass(win_ref, sh_ref, c, after):
    cs = slice(c * LANES, (c + 1) * LANES)
    win = win_ref[:, cs]
    if after is not None:
        win = win + _zero_like_tile(after)[0:1, :]
    n = win.shape[0]
    tok = None
    for shift in range(1, SUBLANES):
        rolled = pltpu.roll(win, n - shift, 0)
        sh_ref[shift - 1, :, cs] = rolled
        tok = rolled[0:SUBLANES]
    return tok


def _conv_unit(win_ref, sh_ref, w_ref, b_ref, y_ref, r0, c, after):
    off = CONV_HALO - (CONV_WIDTH - 1)
    cs = slice(c * LANES, (c + 1) * LANES)
    bias = b_ref[:, cs]
    if after is not None:
        bias = bias + _zero_like_tile(after)[0:1, :]
    acc = jnp.zeros((CONV_ROWS, LANES), F32) + bias
    for j in range(CONV_WIDTH):
        shift, base = (off + j) % SUBLANES, (off + j) // SUBLANES * SUBLANES
        rows = slice(r0 + base, r0 + base + CONV_ROWS)
        src = win_ref[rows, cs] if shift == 0 else sh_ref[shift - 1, rows, cs]
        acc = acc + src * w_ref[j:j + 1, cs]
    y_ref[r0:r0 + CONV_ROWS, cs] = acc
    return acc[CONV_ROWS - SUBLANES:CONV_ROWS, :]


def _back_kernel(x_ref, at_ref, u_ref, halo_ref, mhalo_ref, cw_ref, cb_ref, lg_ref, lb_ref, g_ref, wg_ref,
                 wao_ref, wco_ref, bco_ref, wout_ref, g2_ref, wup_ref, wdn_ref, gf_ref, o_ref,
                 win_ref, y_ref, cact_ref, sh_ref, *, tiles_per_row, n_tiles):
    tm = x_ref.shape[0]
    t = pl.program_id(0)

    @pl.when(t == 0)
    def _():
        cact_ref[...] = jnp.zeros_like(cact_ref)

    row_start = lax.rem(jnp.minimum(t, n_tiles - 1), tiles_per_row) == 0
    win_ref[0:CONV_HALO, :] = jnp.where(row_start, mhalo_ref[...], halo_ref[...])
    win_ref[CONV_HALO:CONV_HALO + tm, :] = u_ref[...]
    units = [("shift", c) for c in range(D_CONV // LANES)]
    units += [(r0, c) for c in range(D_CONV // LANES) for r0 in range(0, tm, CONV_ROWS)]

    def conv_group(n_units, after):
        tok = after
        for _ in range(n_units):
            r0, c = units.pop(0)
            if r0 == "shift":
                tok = _shift_pass(win_ref, sh_ref, c, tok)
            else:
                tok = _conv_unit(win_ref, sh_ref, cw_ref, cb_ref, y_ref, r0, c, tok)
        return tok

    last_tile = lambda v: v[v.shape[0] - SUBLANES:, v.shape[1] - LANES:]
    zeros_bf16 = lambda tok, shape: jnp.tile(
        jnp.concatenate([_zero_like_tile(tok)] * 2, axis=0).astype(BF16),
        (shape[0] // BF16_ROWS, shape[1] // LANES))
    slot = lax.rem(t, 2)
    cact_prev = cact_ref[1 - slot]

    def after(tok, v):
        return v if tok is None else v + zeros_bf16(tok, v.shape)

    x = x_ref[...]
    hn = _rms_norm(x, g_ref[...]).astype(BF16)
    ga_raw = jnp.dot(hn, wg_ref[:, 0:D_MODEL], preferred_element_type=F32)
    tok = conv_group(CONV_GROUPS[0], None)
    gc_raw = jnp.dot(after(tok, hn), wg_ref[:, D_MODEL:2 * D_MODEL], preferred_element_type=F32)
    tok = conv_group(CONV_GROUPS[1], last_tile(ga_raw))
    a = jnp.dot(after(tok, at_ref[...]), wao_ref[...], preferred_element_type=F32)
    tok = conv_group(CONV_GROUPS[2], last_tile(gc_raw))
    c = jnp.dot(after(tok, cact_prev), wco_ref[...], preferred_element_type=F32) + bco_ref[...]
    tok = conv_group(CONV_GROUPS[3], last_tile(a))
    mix = (_sigmoid(ga_raw) * a + _sigmoid(gc_raw) * c).astype(BF16)
    h1 = x + jnp.dot(after(tok, mix), wout_ref[...], preferred_element_type=F32)
    tok = conv_group(CONV_GROUPS[4], last_tile(c))
    hn2 = _rms_norm(h1, g2_ref[...]).astype(BF16)
    prev_result = last_tile(h1)
    h2 = h1
    n_pieces = 4
    piece = D_FF // n_pieces
    acts = []
    for i in range(n_pieces):
        up = jnp.dot(after(tok, hn2), wup_ref[:, i * piece:(i + 1) * piece], preferred_element_type=F32)
        tok = conv_group(CONV_GROUPS[5 + i], prev_result)
        prev_result = last_tile(up)
        acts.append(jnp.square(jnp.maximum(up, 0.0)).astype(BF16))
    for i in range(n_pieces):
        dn = jnp.dot(after(tok, acts[i]), wdn_ref[i * piece:(i + 1) * piece, :], preferred_element_type=F32)
        tok = conv_group(CONV_GROUPS[9 + i], prev_result)
        prev_result = last_tile(dn)
        h2 = h2 + dn
    assert not units
    y = y_ref[...]
    mu = jnp.mean(y, axis=-1, keepdims=True)
    yc = y - mu
    var = jnp.mean(yc * yc, axis=-1, keepdims=True)
    z = yc * lax.rsqrt(var + LN_EPS) * lg_ref[...] + lb_ref[...]
    cact_ref[slot] = (z * _sigmoid(z)).astype(BF16)
    o_ref[...] = _rms_norm(h2, gf_ref[...])


def _back(x, attn, u, meta_halo, consts, tm):
    b, s, d = x.shape
    nj = s // tm
    n_tiles = b * nj
    halo_blocks = tm // CONV_HALO
    prev = lambda t: jnp.maximum(t - 1, 0)
    cur = lambda t: jnp.minimum(t, n_tiles - 1)
    chain = lambda: pl.BlockSpec((None, tm, d), lambda t: (prev(t) // nj, prev(t) % nj, 0))
    return pl.pallas_call(
        functools.partial(_back_kernel, tiles_per_row=nj, n_tiles=n_tiles),
        grid=(n_tiles + 1,),
        in_specs=[chain(), chain(),
                  pl.BlockSpec((None, tm, d), lambda t: (cur(t) // nj, cur(t) % nj, 0)),
                  pl.BlockSpec((None, CONV_HALO, d),
                               lambda t: (cur(t) // nj, jnp.maximum(cur(t) % nj * halo_blocks - 1, 0), 0)),
                  _const_spec(meta_halo.shape)] + [_const_spec(c.shape) for c in consts],
        out_specs=chain(),
        out_shape=jax.ShapeDtypeStruct((b, s, d), F32),
        scratch_shapes=[pltpu.VMEM((CONV_HALO + tm, d), F32), pltpu.VMEM((tm, d), F32),
                        pltpu.VMEM((2, tm, d), BF16), pltpu.VMEM((SUBLANES - 1, CONV_HALO + tm, d), F32)],
        compiler_params=pltpu.CompilerParams(dimension_semantics=("arbitrary",),
                                             vmem_limit_bytes=VMEM_LIMIT),
        name="conv_merge_mlp",
    )(x, attn, u, u, meta_halo, *consts)


def kernel(x, meta_tokens, norm_mix_gain, w_in, b_forget, w_attn_out, b_glu, conv_dw_w, conv_dw_b,
           conv_ln_gain, conv_ln_bias, w_conv_out, b_conv_out, w_out, norm_mlp_gain, w_mlp_up,
           w_mlp_down, final_norm_gain):
    assert w_in.shape[0] == 1, "single layer"
    o_v, o_fg, o_glu = 2 * D_ATTN, 3 * D_ATTN, 3 * D_ATTN + N_HEADS
    o_gate = o_glu + 2 * D_CONV
    w = w_in[0].astype(BF16)
    wqk = w[:, 0:o_v]
    wvt = w[:, o_v:o_fg].T
    lane_pad = LANES - N_SPLIT * N_HEADS
    wfg = jnp.pad(jnp.tile(w[:, o_fg:o_glu], (1, N_SPLIT)), ((0, 0), (0, lane_pad)))
    bfg = jnp.pad(jnp.tile(b_forget[0], N_SPLIT), (0, lane_pad))[None, :]
    wglu = w[:, o_glu:o_gate]
    wgate = w[:, o_gate:]
    g_mix = norm_mix_gain[0][None, :]
    bglu = b_glu[0][None, :]

    proj = functools.partial(_in_proj, g=g_mix, wqk=wqk, wvt=wvt, wfg=wfg, bfg=bfg, wglu=wglu, bglu=bglu)
    _, k_m, vt_m, ncp_m, u_m, carry_m = proj(meta_tokens[None].astype(x.dtype), jnp.zeros((1, LANES), F32),
                                              tm=N_META)
    q, k, vt, ncp, u, _ = proj(x, carry_m[0], tm=IN_PROJ_ROWS)

    attn = _attention(q, k, ncp, vt, k_m[0], ncp_m[0], vt_m[0], tq=ATTN_QUERY_BLOCK)

    meta_halo = jnp.concatenate([jnp.zeros((CONV_HALO - N_META, D_CONV), F32), u_m[0]], axis=0)
    vec = lambda v: v[None, :]
    consts = (conv_dw_w[0], vec(conv_dw_b[0]), vec(conv_ln_gain[0]), vec(conv_ln_bias[0]), g_mix, wgate,
              w_attn_out[0].astype(BF16), w_conv_out[0].astype(BF16), vec(b_conv_out[0]), w_out[0].astype(BF16),
              vec(norm_mlp_gain[0]), w_mlp_up[0].astype(BF16), w_mlp_down[0].astype(BF16), vec(final_norm_gain))
    return _back(x, attn, u, meta_halo, consts, tm=BACK_ROWS)
```
